```python
import math
import jax, jax.numpy as jnp
from jax import lax
import numpy as np

D_MODEL = 2048
BATCH = 4
SEQ = 2048
DEPTH = 2
DEC_BATCH = 8
DEC_SEQ = 8
PAST_LEN = 16384
PAGE_SIZE = 128

N_A_LAYERS = DEPTH // 2
N_B_LAYERS = DEPTH - N_A_LAYERS
HEAD_SIZE_A = 64
N_HEADS_A = D_MODEL // HEAD_SIZE_A
D_DECAY_LORA = max(32, int(round(1.8 * math.sqrt(D_MODEL) / 32)) * 32)
D_AAA_LORA = max(32, int(round(1.8 * math.sqrt(D_MODEL) / 32)) * 32)
HEAD_DIM_B = 128
N_HEADS_B = D_MODEL // HEAD_DIM_B
Q_BLOCK = 128
RMS_EPS = 1e-6
LNX_EPS = 64e-5

kernel_name = "rwkv7_yoco_stickbreaking_step"


def _rmsnorm(x, g):
    xf = x.astype(jnp.float32)
    y = xf * lax.rsqrt(jnp.mean(xf * xf, axis=-1, keepdims=True) + RMS_EPS)
    return (y * g.astype(jnp.float32)).astype(x.dtype)


def _wkv7_scan(S0, r, w, k, v, kk, a):
    def step(S, inp):
        r_t, w_t, k_t, v_t, kk_t, a_t = inp
        sa = jnp.einsum("bhvk,bhk->bhv", S, -kk_t)
        S = (S * w_t[:, :, None, :] + sa[..., None] * (kk_t * a_t)[:, :, None, :]
             + v_t[..., None] * k_t[:, :, None, :])
        o_t = jnp.einsum("bhvk,bhk->bhv", S, r_t)
        return S, o_t
    seq = tuple(jnp.moveaxis(t, 1, 0) for t in (r, w, k, v, kk, a))
    S, o = lax.scan(step, S0, seq)
    return jnp.moveaxis(o, 0, 1), S


def _rwkv7_mix(h, shift_prev, S0, mu, w_rkvg, w0, w1, w2, a0, a1, a2,
               k_k, k_a, r_k, lnx_g, lnx_b, w_o):
    B, T, D = h.shape
    f32 = jnp.float32
    H, N = N_HEADS_A, HEAD_SIZE_A
    h_prev = jnp.concatenate([shift_prev[:, None, :].astype(h.dtype), h[:, :-1]], axis=1)
    xs = h[None] + (h_prev - h)[None] * mu[:, None, None, :].astype(h.dtype)
    r, k, v, g = jnp.einsum("pbtd,pde->pbte", xs[:4], w_rkvg)
    w_raw = -jax.nn.softplus(-(w0 + jnp.tanh(xs[4] @ w1) @ w2).astype(f32)) - 0.5
    decay = jnp.exp(-jnp.exp(w_raw))
    a = jax.nn.sigmoid((a0 + (xs[5] @ a1) @ a2).astype(f32))
    heads = lambda t: t.astype(f32).reshape(B, T, H, N)
    r, k, v, decay, a = heads(r), heads(k), heads(v), heads(decay), heads(a)
    kk = k * k_k.astype(f32).reshape(H, N)
    kk = kk / jnp.maximum(jnp.sqrt(jnp.sum(kk * kk, axis=-1, keepdims=True)), 1e-12)
    k = k * (1.0 + (a - 1.0) * k_a.astype(f32).reshape(H, N))
    o, S = _wkv7_scan(S0.astype(f32), r, decay, k, v, kk, a)
    mean = jnp.mean(o, axis=-1, keepdims=True)
    var = jnp.mean(jnp.square(o - mean), axis=-1, keepdims=True)
    o = ((o - mean) * lax.rsqrt(var + LNX_EPS) * lnx_g.astype(f32).reshape(H, N)
         + lnx_b.astype(f32).reshape(H, N))
    o = o + jnp.sum(r * k * r_k.astype(f32), axis=-1, keepdims=True) * v
    o = o.reshape(B, T, D).astype(h.dtype) * jax.nn.silu(g)
    return o @ w_o, h[:, -1], S.astype(S0.dtype)


def _shared_kv(h, kv_norm, w_kv):
    B, T, _ = h.shape
    k, v = jnp.split(_rmsnorm(h, kv_norm) @ w_kv, 2, axis=-1)
    return (k.reshape(B, T, N_HEADS_B, HEAD_DIM_B), v.reshape(B, T, N_HEADS_B, HEAD_DIM_B))


def _stick_breaking(q, k, v, q_pos, k_pos, logit_bias):
    f32 = jnp.float32
    B, Tq, H, d = q.shape
    qb_size = Q_BLOCK if Tq % Q_BLOCK == 0 else Tq
    nb = Tq // qb_size
    scale = 1.0 / math.sqrt(d)
    kf, vf = k.astype(f32), v.astype(f32)
    bias = logit_bias.astype(f32)[None, :, None, None]
    qb = jnp.moveaxis(q.reshape(B, nb, qb_size, H, d), 1, 0)
    pb = q_pos.reshape(nb, qb_size)

    def block(args):
        q_i, p_i = args
        z = jnp.einsum("bqhd,bkhd->bhqk", q_i.astype(f32), kf) * scale + bias
        mask = (k_pos[None, :] < p_i[:, None])[None, None]
        log_keep = jnp.where(mask, jax.nn.log_sigmoid(-z), 0.0)
        log_rest = lax.cumsum(log_keep, axis=3, reverse=True) - log_keep
        att = jnp.where(mask, jnp.exp(jax.nn.log_sigmoid(z) + log_rest), 0.0)
        return jnp.einsum("bhqk,bkhd->bqhd", att, vf)

    o = lax.map(block, (qb, pb))
    return jnp.moveaxis(o, 0, 1).reshape(B, Tq, H, d).astype(q.dtype)


def _sb_mix(h, k, v, q_pos, k_pos, w_qg, logit_bias, w_o):
    B, T, D = h.shape
    q, g = jnp.split(h @ w_qg, 2, axis=-1)
    o = _stick_breaking(q.reshape(B, T, N_HEADS_B, HEAD_DIM_B), k, v, q_pos, k_pos, logit_bias)
    return (o.reshape(B, T, D) * jax.nn.silu(g)) @ w_o


def setup_inputs(seed: int = 0) -> dict:
    key = jax.random.key(seed)
    ks = jax.random.split(key, 32)
    f32 = jnp.float32
    D = D_MODEL
    nrm = lambda i, shape: jax.random.normal(ks[i], shape, f32)
    n_pages = PAST_LEN // PAGE_SIZE
    n_used = DEC_BATCH * n_pages
    n_pool = n_used + max(1, n_used // 4)
    page_table = jax.random.permutation(ks[6], n_pool)[:n_used].reshape(DEC_BATCH, n_pages).astype(jnp.int32)
    return {
        "x_prompt": nrm(0, (BATCH, SEQ, D)),
        "x_sample": nrm(1, (DEC_BATCH, DEC_SEQ, D)),
        "state_shift": nrm(2, (N_A_LAYERS, DEC_BATCH, D)),
        "state_wkv": 0.1 * nrm(3, (N_A_LAYERS, DEC_BATCH, N_HEADS_A, HEAD_SIZE_A, HEAD_SIZE_A)),
        "cache_k": nrm(4, (n_pool, PAGE_SIZE, N_HEADS_B, HEAD_DIM_B)),
        "cache_v": nrm(5, (n_pool, PAGE_SIZE, N_HEADS_B, HEAD_DIM_B)),
        "page_table": page_table,
        "a_norm": 1.0 + 0.02 * nrm(7, (N_A_LAYERS, D)),
        "a_mu": jax.random.uniform(ks[8], (N_A_LAYERS, 6, D), f32),
        "a_w_rkvg": nrm(9, (N_A_LAYERS, 4, D, D)) * D ** -0.5,
        "a_w0": jax.random.uniform(ks[10], (N_A_LAYERS, D), f32, minval=-3.0, maxval=1.0),
        "a_w1": nrm(11, (N_A_LAYERS, D, D_DECAY_LORA)) * D ** -0.5,
        "a_w2": nrm(12, (N_A_LAYERS, D_DECAY_LORA, D)) * 0.1 * D_DECAY_LORA ** -0.5,
        "a_a0": 0.1 * nrm(13, (N_A_LAYERS, D)),
        "a_a1": nrm(14, (N_A_LAYERS, D, D_AAA_LORA)) * D ** -0.5,
        "a_a2": nrm(15, (N_A_LAYERS, D_AAA_LORA, D)) * 0.1 * D_AAA_LORA ** -0.5,
        "a_k_k": 0.85 + 0.05 * nrm(16, (N_A_LAYERS, D)),
        "a_k_a": 1.0 + 0.05 * nrm(17, (N_A_LAYERS, D)),
        "a_r_k": 0.1 * nrm(18, (N_A_LAYERS, N_HEADS_A, HEAD_SIZE_A)),
        "a_lnx_g": 1.0 + 0.02 * nrm(19, (N_A_LAYERS, D)),
        "a_lnx_b": 0.02 * nrm(20, (N_A_LAYERS, D)),
        "a_w_o": nrm(21, (N_A_LAYERS, D, D)) * D ** -0.5,
        "kv_norm": 1.0 + 0.02 * nrm(22, (D,)),
        "w_kv": nrm(23, (D, 2 * D)) * D ** -0.5,
        "b_norm": 1.0 + 0.02 * nrm(24, (N_B_LAYERS, D)),
        "b_w_qg": nrm(25, (N_B_LAYERS, D, 2 * D)) * D ** -0.5,
        "b_logit_bias": jax.random.uniform(ks[28], (N_B_LAYERS, N_HEADS_B), f32, minval=-10.0, maxval=-4.0),
        "b_w_o": nrm(26, (N_B_LAYERS, D, D)) * D ** -0.5,
        "final_norm": 1.0 + 0.02 * nrm(27, (D,)),
    }


def reference(x_prompt, x_sample, state_shift, state_wkv, cache_k, cache_v, page_table,
              a_norm, a_mu, a_w_rkvg, a_w0, a_w1, a_w2, a_a0, a_a1, a_a2, a_k_k, a_k_a,
              a_r_k, a_lnx_g, a_lnx_b, a_w_o, kv_norm, w_kv, b_norm, b_w_qg, b_logit_bias,
              b_w_o, final_norm):
    B, T, D = x_prompt.shape
    Bs, Ts, _ = x_sample.shape
    n_pages = page_table.shape[1]
    past_len = n_pages * cache_k.shape[1]
    pos_prompt = jnp.arange(T, dtype=jnp.int32)
    k_pos_sample = jnp.arange(past_len + Ts, dtype=jnp.int32)
    q_pos_sample = past_len + jnp.arange(Ts, dtype=jnp.int32)

    h_p, h_s = x_prompt, x_sample
    shift_p, wkv_p, shift_s, wkv_s = [], [], [], []
    k_p = v_p = k_s_new = v_s_new = k_s_all = v_s_all = None
    for i in range(DEPTH):
        if i < N_A_LAYERS:
            w_args = (a_mu[i], a_w_rkvg[i], a_w0[i], a_w1[i], a_w2[i], a_a0[i], a_a1[i], a_a2[i],
                      a_k_k[i], a_k_a[i], a_r_k[i], a_lnx_g[i], a_lnx_b[i], a_w_o[i])
            out_p, sh_p, S_p = _rwkv7_mix(_rmsnorm(h_p, a_norm[i]),
                                          jnp.zeros((B, D), x_prompt.dtype),
                                          jnp.zeros((B,) + state_wkv.shape[2:], state_wkv.dtype),
                                          *w_args)
            out_s, sh_s, S_s = _rwkv7_mix(_rmsnorm(h_s, a_norm[i]), state_shift[i], state_wkv[i],
                                          *w_args)
            h_p = h_p + out_p
            h_s = h_s + out_s
            shift_p.append(sh_p); wkv_p.append(S_p)
            shift_s.append(sh_s); wkv_s.append(S_s)
        else:
            if i == N_A_LAYERS:
                k_p, v_p = _shared_kv(h_p, kv_norm, w_kv)
                k_s_new, v_s_new = _shared_kv(h_s, kv_norm, w_kv)
                k_past = cache_k[page_table].reshape(Bs, past_len, N_HEADS_B, HEAD_DIM_B)
                v_past = cache_v[page_table].reshape(Bs, past_len, N_HEADS_B, HEAD_DIM_B)
                k_s_all = jnp.concatenate([k_past.astype(k_s_new.dtype), k_s_new], axis=1)
                v_s_all = jnp.concatenate([v_past.astype(v_s_new.dtype), v_s_new], axis=1)
            j = i - N_A_LAYERS
            h_p = h_p + _sb_mix(_rmsnorm(h_p, b_norm[j]), k_p, v_p, pos_prompt, pos_prompt,
                                b_w_qg[j], b_logit_bias[j], b_w_o[j])
            h_s = h_s + _sb_mix(_rmsnorm(h_s, b_norm[j]), k_s_all, v_s_all, q_pos_sample,
                                k_pos_sample, b_w_qg[j], b_logit_bias[j], b_w_o[j])

    y_prompt = _rmsnorm(h_p, final_norm)
    y_sample = _rmsnorm(h_s, final_norm)
    new_shift_prompt = jnp.stack(shift_p)
    new_wkv_prompt = jnp.stack(wkv_p)
    new_shift_sample = jnp.stack(shift_s)
    new_wkv_sample = jnp.stack(wkv_s)
    return (y_prompt, y_sample, new_shift_prompt, new_wkv_prompt, k_p, v_p,
            new_shift_sample, new_wkv_sample, k_s_new, v_s_new)
```

```python
import functools
import math

import jax
import jax.numpy as jnp
from jax import lax
from jax.experimental import pallas as pl
from jax.experimental.pallas import tpu as pltpu

F32 = jnp.float32
BF16 = jnp.bfloat16

HEAD_A = 64
HEAD_A_SHIFT = 6
HEAD_B = 128
LANES = 128
SUBLANES = 8
RMS_EPS = 1e-6
LNX_EPS = 64e-5
LORA_PAD = 128
WKV_CHUNK = 64
VMEM_LIMIT = 56 * 1024 * 1024

NN = (((1,), (0,)), ((), ()))
NT = (((1,), (1,)), ((), ()))
TN = (((0,), (0,)), ((), ()))


def _split(x, n):
    parts = []
    for _ in range(n - 1):
        hi = x.astype(BF16)
        parts.append(hi)
        x = x - hi.astype(F32)
    parts.append(x.astype(BF16))
    return parts


def _dot(a, b, dims=NN, pa=1, pb=1):
    ap = _split(a, pa) if pa > 1 or a.dtype != BF16 else [a]
    bp = _split(b, pb) if pb > 1 or b.dtype != BF16 else [b]
    n = max(pa, pb)
    acc = None
    for i in reversed(range(len(ap))):
        for j in reversed(range(len(bp))):
            if i + j < n:
                t = lax.dot_general(ap[i], bp[j], dims, preferred_element_type=F32)
                acc = t if acc is None else acc + t
    return acc


def _softplus(z):
    return jnp.maximum(z, 0.0) + jnp.log(1.0 + jnp.exp(-jnp.abs(z)))


def _silu(g):
    return g / (1.0 + jnp.exp(-g))


def _rms(x, gain):
    return x * lax.rsqrt(jnp.mean(x * x, axis=-1, keepdims=True) + RMS_EPS) * gain


def _params(*sem):
    return pltpu.CompilerParams(dimension_semantics=sem, vmem_limit_bytes=VMEM_LIMIT)


def _nsl_kernel(x_ref, xh_ref, sh_ref, gain_ref, mu_ref, w1_ref, w2_ref, w0_ref,
                a1_ref, a2_ref, a0_ref, xs_ref, h_ref, lw_ref, a_ref, *, bm, seq, per_row_start):
    i = pl.program_id(0)
    gain = gain_ref[...]
    h = _rms(x_ref[...], gain)
    rolled = pltpu.roll(h, 1, 0)
    row = lax.broadcasted_iota(jnp.int32, (bm, 1), 0)
    if per_row_start:
        h_prev = jnp.where((row & (seq - 1)) == 0, sh_ref[...], rolled)
    else:
        halo = _rms(xh_ref[0], gain)
        first = jnp.where((i * bm) % seq == 0, sh_ref[...], halo[SUBLANES - 1:SUBLANES])
        h_prev = jnp.where(row == 0, first, rolled)
    dx = h_prev - h
    h_ref[...] = h
    for p in range(4):
        xs_ref[p] = (h + dx * mu_ref[p:p + 1, :]).astype(BF16)
    x4 = (h + dx * mu_ref[4:5, :]).astype(BF16)
    t = jnp.tanh(jnp.dot(x4, w1_ref[...], preferred_element_type=F32)).astype(BF16)
    wl = jnp.dot(t, w2_ref[...], preferred_element_type=F32) + w0_ref[...]
    w_raw = jnp.minimum(wl, 0.0) - jnp.log(1.0 + jnp.exp(-jnp.abs(wl))) - 0.5
    lw_ref[...] = -jnp.exp(w_raw)
    x5 = (h + dx * mu_ref[5:6, :]).astype(BF16)
    u = jnp.dot(x5, a1_ref[...], preferred_element_type=F32).astype(BF16)
    al = jnp.dot(u, a2_ref[...], preferred_element_type=F32) + a0_ref[...]
    a_ref[...] = 1.0 / (1.0 + jnp.exp(-al))


def _norm_shift_lora(x, shift, seq, gain, mu, w1, w2, w0, a1, a2, a0):
    m, d = x.shape
    per_row_start = seq < SUBLANES * 2
    bm = m if per_row_start else min(256, seq)
    assert m % bm == 0 and (per_row_start or seq % bm == 0)
    nb = m // seq
    if per_row_start:
        sh = jnp.repeat(shift, seq, axis=0)
        sh_spec = pl.BlockSpec((bm, d), lambda i: (i, 0))
    else:
        sh = shift.reshape(nb, 1, d)
        sh_spec = pl.BlockSpec((None, 1, d), lambda i: ((i * bm) // seq, 0, 0))
    xh = x.reshape(m // SUBLANES, SUBLANES, d)
    row = lambda i: (i, 0)
    const = lambda i: (0, 0)
    kern = functools.partial(_nsl_kernel, bm=bm, seq=seq, per_row_start=per_row_start)
    return pl.pallas_call(
        kern,
        grid=(m // bm,),
        in_specs=[
            pl.BlockSpec((bm, d), row),
            pl.BlockSpec((1, SUBLANES, d), lambda i: (jnp.maximum(i * (bm // SUBLANES) - 1, 0), 0, 0)),
            sh_spec,
            pl.BlockSpec((1, d), const),
            pl.BlockSpec(mu.shape, const),
            pl.BlockSpec(w1.shape, const),
            pl.BlockSpec(w2.shape, const),
            pl.BlockSpec((1, d), const),
            pl.BlockSpec(a1.shape, const),
            pl.BlockSpec(a2.shape, const),
            pl.BlockSpec((1, d), const),
        ],
        out_specs=[
            pl.BlockSpec((4, bm, d), lambda i: (0, i, 0)),
            pl.BlockSpec((bm, d), row),
            pl.BlockSpec((bm, d), row),
            pl.BlockSpec((bm, d), row),
        ],
        out_shape=[
            jax.ShapeDtypeStruct((4, m, d), BF16),
            jax.ShapeDtypeStruct((m, d), F32),
            jax.ShapeDtypeStruct((m, d), F32),
            jax.ShapeDtypeStruct((m, d), F32),
        ],
        compiler_params=_params("arbitrary"),
        name="norm_shift_lora",
    )(x, xh, sh, gain.reshape(1, d), mu, w1, w2, w0.reshape(1, d), a1, a2, a0.reshape(1, d))


def _bmm_kernel(x_ref, w_ref, o_ref):
    o_ref[...] = jnp.dot(x_ref[...], w_ref[...], preferred_element_type=F32)


def _bmm(x, w):
    p, m, k = x.shape
    n = w.shape[2]
    bm = min(512, m)
    assert m % bm == 0
    return pl.pallas_call(
        _bmm_kernel,
        grid=(p, m // bm),
        in_specs=[
            pl.BlockSpec((None, bm, k), lambda q, i: (q, i, 0)),
            pl.BlockSpec((None, k, n), lambda q, i: (q, 0, 0)),
        ],
        out_specs=pl.BlockSpec((None, bm, n), lambda q, i: (q, i, 0)),
        out_shape=jax.ShapeDtypeStruct((p, m, n), F32),
        compiler_params=_params("arbitrary", "arbitrary"),
        name="bmm",
    )(x, w)


def _wkv_kernel(r_ref, k_ref, v_ref, g_ref, w_ref, a_ref, kk_ref, ka_ref, rk_ref, lg_ref, lb_ref,
                s0_ref, o_ref, so_ref, y_ref, *, chunk, rows):
    c = pl.program_id(2)
    nc = pl.num_programs(2)
    C = chunk
    C2 = 2 * C

    @pl.when(c == 0)
    def _():
        y_ref[...] = s0_ref[...]

    def load(ref):
        x = ref[...]
        if rows < C:
            x = jnp.concatenate([x, jnp.zeros((C - rows, LANES), F32)], axis=0)
        return x

    R, K, V, Wl, A = load(r_ref), load(k_ref), load(v_ref), load(w_ref), load(a_ref)

    lane = lax.broadcasted_iota(jnp.int32, (1, LANES), 1)
    head0 = lane < HEAD_A
    ri = lax.broadcasted_iota(jnp.int32, (LANES, LANES), 0)
    ci = lax.broadcasted_iota(jnp.int32, (LANES, LANES), 1)
    same_head = (ri >> HEAD_A_SHIFT) == (ci >> HEAD_A_SHIFT)
    ones_bd = jnp.where(same_head, 1.0, 0.0).astype(BF16)

    def stack(x):
        return jnp.concatenate([jnp.where(head0, x, 0.0), jnp.where(head0, 0.0, x)], axis=0)

    def pack(x):
        return x[:C] + x[C:]

    kkp = K * kk_ref[...]
    n2 = _dot(kkp * kkp, ones_bd, pa=2)
    kk = kkp / jnp.maximum(jnp.sqrt(n2), 1e-12)
    Km = K * (1.0 + (A - 1.0) * ka_ref[...])

    ti = lax.broadcasted_iota(jnp.int32, (C, C), 0)
    tj = lax.broadcasted_iota(jnp.int32, (C, C), 1)
    tril = jnp.where(tj <= ti, 1.0, 0.0).astype(BF16)
    Lc = _dot(tril, Wl, pb=3)
    Lend = Lc[C - 1:C, :]
    g_inv = jnp.exp(-Lc)
    g_rem = jnp.exp(Lend - Lc)
    kka = kk * A
    Ah = -kk * jnp.exp(Lc - Wl)
    Rh = R * jnp.exp(Lc)
    Bc = kka * g_inv
    Kc = Km * g_inv
    Bt = kka * g_rem
    Kt = Km * g_rem

    sAh = stack(Ah)
    lhs = jnp.concatenate([sAh, stack(Rh)], axis=0)
    rhs = jnp.concatenate([stack(Bc), stack(Kc)], axis=0)
    mx = _dot(lhs, rhs, NT, pa=2, pb=2)
    ri2 = lax.broadcasted_iota(jnp.int32, (C2, C2), 0)
    ci2 = lax.broadcasted_iota(jnp.int32, (C2, C2), 1)
    si = ri2 & (C - 1)
    sj = ci2 & (C - 1)
    strict = sj < si
    incl = sj <= si
    Nab = jnp.where(strict, mx[:C2, :C2], 0.0)
    Nak = jnp.where(strict, mx[:C2, C2:], 0.0)
    Nrb = jnp.where(incl, mx[C2:, :C2], 0.0)
    Nrk = jnp.where(incl, mx[C2:, C2:], 0.0)

    eye = jnp.where(ri2 == ci2, 1.0, 0.0)
    Tm = eye + Nab
    Pw = Nab
    for _ in range(int(math.log2(C)) - 1):
        Pw = _dot(Pw, Pw, pa=2, pb=2)
        Tm = Tm + _dot(Tm, Pw, pa=2, pb=2)

    sV = stack(V)
    x2 = _dot(Nak, sV, pa=2, pb=2)
    tx = _dot(Tm, jnp.concatenate([sAh, x2], axis=1), pa=2, pb=2)
    o_intra = _dot(Nrk, sV, pa=2, pb=2)
    gk = jnp.where(same_head, _dot(V, Kt, TN, pa=2, pb=2), 0.0)

    Y = y_ref[...]
    Ub = _dot(tx[:, :LANES], Y, NT, pa=2, pb=2) + tx[:, LANES:]
    U = pack(Ub)
    O = _dot(Rh, Y, NT, pa=2, pb=2) + pack(_dot(Nrb, Ub, pa=2, pb=2) + o_intra)
    y_new = Y * jnp.exp(Lend) + jnp.where(same_head, _dot(U, Bt, TN, pa=2, pb=2), 0.0) + gk
    y_ref[...] = y_new

    @pl.when(c == nc - 1)
    def _():
        so_ref[...] = y_new

    avg_bd = jnp.where(same_head, 1.0 / HEAD_A, 0.0).astype(BF16)
    mean = _dot(O, avg_bd, pa=3)
    dlt = O - mean
    var = _dot(dlt * dlt, avg_bd, pa=3)
    on = dlt * lax.rsqrt(var + LNX_EPS) * lg_ref[...] + lb_ref[...]
    bonus = _dot(R * Km * rk_ref[...], ones_bd, pa=3) * V
    out = (on + bonus).astype(F32)
    G = load(g_ref)
    res = out * _silu(G)
    o_ref[...] = res[:rows].astype(o_ref.dtype)


def _wkv(rkvg, lw, a, s0_bd, seq, k_k, k_a, r_k, lnx_g, lnx_b):
    _, m, d = rkvg.shape
    nb = m // seq
    npair = d // LANES
    rows = min(WKV_CHUNK, seq)
    assert seq % rows == 0 and rows % SUBLANES == 0
    nc = seq // rows
    tok = lambda b, h, c: (b * nc + c, h)
    par = lambda b, h, c: (0, h)
    st = lambda b, h, c: (b, h, 0, 0)

    def proj(p):
        return pl.BlockSpec((None, rows, LANES), lambda b, h, c: (p, b * nc + c, h))

    vec = lambda x: x.reshape(1, d)
    kern = functools.partial(_wkv_kernel, chunk=WKV_CHUNK, rows=rows)
    return pl.pallas_call(
        kern,
        grid=(nb, npair, nc),
        in_specs=[proj(0), proj(1), proj(2), proj(3),
                  pl.BlockSpec((rows, LANES), tok), pl.BlockSpec((rows, LANES), tok),
                  pl.BlockSpec((1, LANES), par), pl.BlockSpec((1, LANES), par),
                  pl.BlockSpec((1, LANES), par), pl.BlockSpec((1, LANES), par),
                  pl.BlockSpec((1, LANES), par),
                  pl.BlockSpec((None, None, LANES, LANES), st)],
        out_specs=[pl.BlockSpec((rows, LANES), tok),
                   pl.BlockSpec((None, None, LANES, LANES), st)],
        out_shape=[jax.ShapeDtypeStruct((m, d), BF16 if rows % (2 * SUBLANES) == 0 else F32),
                   jax.ShapeDtypeStruct((nb, npair, LANES, LANES), F32)],
        scratch_shapes=[pltpu.VMEM((LANES, LANES), F32)],
        compiler_params=_params("arbitrary", "arbitrary", "arbitrary"),
        name="wkv",
    )(rkvg, rkvg, rkvg, rkvg, lw, a, vec(k_k), vec(k_a), vec(r_k), vec(lnx_g), vec(lnx_b), s0_bd)


def _state_to_pairs(s):
    nb, nh, n, _ = s.shape
    s = s.reshape(nb, nh // 2, 2, n, n)
    z = jnp.zeros_like(s[:, :, 0])
    top = jnp.concatenate([s[:, :, 0], z], axis=-1)
    bot = jnp.concatenate([z, s[:, :, 1]], axis=-1)
    return jnp.concatenate([top, bot], axis=-2)


def _pairs_to_state(y):
    nb, npair = y.shape[:2]
    n = HEAD_A
    s = jnp.stack([y[:, :, :n, :n], y[:, :, n:, n:]], axis=2)
    return s.reshape(nb, 2 * npair, n, n)


def _mm_res_kernel(x_ref, w_ref, r_ref, g_ref, o_ref, *, final_norm):
    y = r_ref[...] + jnp.dot(x_ref[...].astype(BF16), w_ref[...], preferred_element_type=F32)
    if final_norm:
        y = _rms(y, g_ref[...])
    o_ref[...] = y


def _mm_res(x, w, res, gain=None):
    m, k = x.shape
    n = w.shape[1]
    bm = min(256, m)
    assert m % bm == 0
    final_norm = gain is not None
    g = (gain if final_norm else jnp.ones((n,), F32)).reshape(1, n)
    return pl.pallas_call(
        functools.partial(_mm_res_kernel, final_norm=final_norm),
        grid=(m // bm,),
        in_specs=[pl.BlockSpec((bm, k), lambda i: (i, 0)),
                  pl.BlockSpec((k, n), lambda i: (0, 0)),
                  pl.BlockSpec((bm, n), lambda i: (i, 0)),
                  pl.BlockSpec((1, n), lambda i: (0, 0))],
        out_specs=pl.BlockSpec((bm, n), lambda i: (i, 0)),
        out_shape=jax.ShapeDtypeStruct((m, n), F32),
        compiler_params=_params("arbitrary"),
        name="mm_res",
    )(x, w, res, g)


def _norm_mm_kernel(x_ref, g_ref, w_ref, o_ref):
    xn = _rms(x_ref[...], g_ref[...]).astype(BF16)
    o_ref[...] = jnp.dot(xn, w_ref[...], preferred_element_type=F32)


def _norm_mm(x, gain, w):
    m, k = x.shape
    n = w.shape[1]
    bm = min(256, m)
    assert m % bm == 0
    return pl.pallas_call(
        _norm_mm_kernel,
        grid=(m // bm,),
        in_specs=[pl.BlockSpec((bm, k), lambda i: (i, 0)),
                  pl.BlockSpec((1, k), lambda i: (0, 0)),
                  pl.BlockSpec((k, n), lambda i: (0, 0))],
        out_specs=pl.BlockSpec((bm, n), lambda i: (i, 0)),
        out_shape=jax.ShapeDtypeStruct((m, n), F32),
        compiler_params=_params("arbitrary"),
        name="norm_mm",
    )(x, gain.reshape(1, k), w)


SB_BLOCK = 256
SB_CUM_PIECES = 2


def _sbp_kernel(bias_ref, q_ref, k_ref, v_ref, g_ref, o_ref, kb_ref, vb_ref, *, blk, scale):
    h = pl.program_id(1)
    qi = pl.program_id(2)

    @pl.when(qi == 0)
    def _():
        kb_ref[...] = k_ref[...].astype(BF16)
        vb_ref[...] = v_ref[...].astype(BF16)

    bias = bias_ref[h]
    q = (q_ref[...] * scale).astype(BF16)
    ri = lax.broadcasted_iota(jnp.int32, (blk, blk), 0)
    ci = lax.broadcasted_iota(jnp.int32, (blk, blk), 1)
    suffix = jnp.where(ri > ci, 1.0, 0.0).astype(BF16)
    causal = ci < ri

    def block(j, carry, acc, diag):
        start = pl.multiple_of(j * blk, blk)
        kb = kb_ref[pl.ds(start, blk), :]
        z = lax.dot_general(q, kb, NT, preferred_element_type=F32) + bias
        sp = _softplus(z)
        lk = -sp
        if diag:
            lk = jnp.where(causal, lk, 0.0)
        rest = _dot(lk, suffix, pa=SB_CUM_PIECES)
        att = jnp.exp(z - sp + rest + carry)
        if diag:
            att = jnp.where(causal, att, 0.0)
        acc = acc + jnp.dot(att.astype(BF16), vb_ref[pl.ds(start, blk), :], preferred_element_type=F32)
        carry = carry + jnp.sum(lk, axis=1, keepdims=True)
        return carry, acc

    carry0 = jnp.zeros((blk, 1), F32)
    acc0 = jnp.zeros((blk, HEAD_B), F32)
    carry, acc = block(qi, carry0, acc0, True)

    def body(it, ca):
        return block(qi - 1 - it, ca[0], ca[1], False)

    carry, acc = lax.fori_loop(0, qi, body, (carry, acc))
    o_ref[...] = (acc * _silu(g_ref[...])).astype(o_ref.dtype)


def _sb_prompt(q, k, v, g, bias, seq):
    m, d = q.shape
    nb = m // seq
    nh = d // HEAD_B
    blk = min(SB_BLOCK, seq)
    assert seq % blk == 0
    nq = seq // blk
    qmap = lambda b, h, i: (b * nq + i, h)
    kmap = lambda b, h, i: (b, h)
    kern = functools.partial(_sbp_kernel, blk=blk, scale=1.0 / math.sqrt(HEAD_B))
    return pl.pallas_call(
        kern,
        grid=(nb, nh, nq),
        in_specs=[pl.BlockSpec(memory_space=pltpu.SMEM),
                  pl.BlockSpec((blk, HEAD_B), qmap),
                  pl.BlockSpec((seq, HEAD_B), kmap),
                  pl.BlockSpec((seq, HEAD_B), kmap),
                  pl.BlockSpec((blk, HEAD_B), qmap)],
        out_specs=pl.BlockSpec((blk, HEAD_B), qmap),
        out_shape=jax.ShapeDtypeStruct((m, d), BF16),
        scratch_shapes=[pltpu.VMEM((seq, HEAD_B), BF16), pltpu.VMEM((seq, HEAD_B), BF16)],
        compiler_params=_params("arbitrary", "arbitrary", "arbitrary"),
        name="sb_prompt",
    )(bias, q, k, v, g)


def _sbs_kernel(pt_ref, qbd_ref, bias_ref, kn_ref, vn_ref, kp_ref, vp_ref, g_ref, o_ref,
                acc_ref, carry_ref, *, ts, page, nh):
    del pt_ref
    p = pl.program_id(1)
    last = pl.num_programs(1) - 1
    cols = nh * ts
    qbd = qbd_ref[...]
    bias = bias_ref[...]
    ri = lax.broadcasted_iota(jnp.int32, (page, page), 0)
    ci = lax.broadcasted_iota(jnp.int32, (page, page), 1)
    suffix = jnp.where(ci > ri, 1.0, 0.0).astype(BF16)

    def process(kf, vf, mask):
        z = jnp.dot(kf.astype(BF16), qbd, preferred_element_type=F32) + bias
        sp = _softplus(z)
        lk = -sp
        if mask is not None:
            lk = jnp.where(mask, lk, 0.0)
        rest = _dot(suffix, lk, pb=SB_CUM_PIECES)
        att = jnp.exp(z - sp + rest + carry_ref[...])
        if mask is not None:
            att = jnp.where(mask, att, 0.0)
        acc_ref[...] += lax.dot_general(att.astype(BF16), vf.astype(BF16), TN, preferred_element_type=F32)
        carry_ref[...] += jnp.sum(lk, axis=0, keepdims=True)

    @pl.when(p == 0)
    def _():
        acc_ref[...] = jnp.zeros_like(acc_ref)
        carry_ref[...] = jnp.zeros_like(carry_ref)
        pad = jnp.zeros((page - ts, kn_ref.shape[-1]), F32)
        kn = jnp.concatenate([kn_ref[...], pad], axis=0)
        vn = jnp.concatenate([vn_ref[...], pad], axis=0)
        krow = lax.broadcasted_iota(jnp.int32, (page, cols), 0)
        qcol = lax.broadcasted_iota(jnp.int32, (page, cols), 1) & (ts - 1)
        process(kn, vn, krow < qcol)

    @pl.when(p > 0)
    def _():
        process(kp_ref[...], vp_ref[...], None)

    @pl.when(p == last)
    def _():
        g = g_ref[...]
        for h in range(nh):
            sl = slice(h * HEAD_B, (h + 1) * HEAD_B)
            o = acc_ref[h * ts:(h + 1) * ts, sl]
            o_ref[:, sl] = (o * _silu(g[:, sl])).astype(o_ref.dtype)


def _sb_sample(q, k_new, v_new, g, cache_k, cache_v, page_table, bias, ts):
    m, d = q.shape
    nb = m // ts
    nh = d // HEAD_B
    n_pool, page = cache_k.shape[:2]
    n_pages = page_table.shape[1]
    cols = nh * ts
    assert ts % SUBLANES == 0 and ts & (ts - 1) == 0
    scale = 1.0 / math.sqrt(HEAD_B)
    q4 = (q * scale).reshape(nb, ts, nh, HEAD_B)
    eye = jnp.eye(nh, dtype=F32)
    qbd = jnp.einsum("bihc,hg->bhcgi", q4, eye).reshape(nb, d, cols).astype(BF16)
    bias_cols = jnp.repeat(bias, ts).reshape(1, cols)
    ck = cache_k.reshape(n_pool, page, d)
    cv = cache_v.reshape(n_pool, page, d)
    tok = lambda b, p, pt: (b, 0, 0)
    pg = lambda b, p, pt: (pt[b, n_pages - jnp.maximum(p, 1)], 0, 0)
    grid_spec = pltpu.PrefetchScalarGridSpec(
        num_scalar_prefetch=1,
        grid=(nb, n_pages + 1),
        in_specs=[pl.BlockSpec((None, d, cols), tok),
                  pl.BlockSpec((1, cols), lambda b, p, pt: (0, 0)),
                  pl.BlockSpec((None, ts, d), tok),
                  pl.BlockSpec((None, ts, d), tok),
                  pl.BlockSpec((None, page, d), pg),
                  pl.BlockSpec((None, page, d), pg),
                  pl.BlockSpec((None, ts, d), tok)],
        out_specs=pl.BlockSpec((None, ts, d), tok),
        scratch_shapes=[pltpu.VMEM((cols, d), F32), pltpu.VMEM((1, cols), F32)],
    )
    kern = functools.partial(_sbs_kernel, ts=ts, page=page, nh=nh)
    out = pl.pallas_call(
        kern,
        grid_spec=grid_spec,
        out_shape=jax.ShapeDtypeStruct((nb, ts, d), F32),
        compiler_params=_params("arbitrary", "arbitrary"),
        name="sb_sample",
    )(page_table, qbd, bias_cols, k_new.reshape(nb, ts, d), v_new.reshape(nb, ts, d), ck, cv,
      g.reshape(nb, ts, d))
    return out.reshape(m, d)


def _pad_lora(w_in, w_out):
    r = w_in.shape[1]
    w_in = jnp.pad(w_in, ((0, 0), (0, LORA_PAD - r))).astype(BF16)
    w_out = jnp.pad(w_out, ((0, LORA_PAD - r), (0, 0))).astype(BF16)
    return w_in, w_out


def kernel(x_prompt, x_sample, state_shift, state_wkv, cache_k, cache_v, page_table, a_norm, a_mu, a_w_rkvg, a_w0, a_w1, a_w2, a_a0, a_a1, a_a2, a_k_k, a_k_a, a_r_k, a_lnx_g, a_lnx_b, a_w_o, kv_norm, w_kv, b_norm, b_w_qg, b_logit_bias, b_w_o, final_norm):
    nbp, tp, d = x_prompt.shape
    nbs, ts, _ = x_sample.shape
    n_a = a_norm.shape[0]
    n_b = b_norm.shape[0]
    nh_a = d // HEAD_A
    nh_b = d // HEAD_B

    hp = x_prompt.reshape(nbp * tp, d)
    hs = x_sample.reshape(nbs * ts, d)
    shift_p, wkv_p, shift_s, wkv_s = [], [], [], []

    for i in range(n_a):
        w1, w2 = _pad_lora(a_w1[i], a_w2[i])
        a1, a2 = _pad_lora(a_a1[i], a_a2[i])
        w_rkvg = a_w_rkvg[i].astype(BF16)
        w_o = a_w_o[i].astype(BF16)
        r_k = a_r_k[i].reshape(d)

        def a_layer(h_in, seq, shift0, s0):
            nb = h_in.shape[0] // seq
            xs, hn, lw, a = _norm_shift_lora(h_in, shift0, seq, a_norm[i], a_mu[i], w1, w2, a_w0[i],
                                             a1, a2, a_a0[i])
            rkvg = _bmm(xs, w_rkvg)
            og, y = _wkv(rkvg, lw, a, _state_to_pairs(s0.astype(F32)), seq,
                         a_k_k[i], a_k_a[i], r_k, a_lnx_g[i], a_lnx_b[i])
            h_out = _mm_res(og, w_o, h_in)
            return h_out, hn.reshape(nb, seq, d)[:, -1], _pairs_to_state(y).astype(s0.dtype)

        hp, sh, st = a_layer(hp, tp, jnp.zeros((nbp, d), F32),
                             jnp.zeros((nbp,) + state_wkv.shape[2:], state_wkv.dtype))
        shift_p.append(sh); wkv_p.append(st)
        hs, sh, st = a_layer(hs, ts, state_shift[i], state_wkv[i])
        shift_s.append(sh); wkv_s.append(st)

    w_k = w_kv[:, :d].astype(BF16)
    w_v = w_kv[:, d:].astype(BF16)
    k_p = _norm_mm(hp, kv_norm, w_k)
    v_p = _norm_mm(hp, kv_norm, w_v)
    k_s = _norm_mm(hs, kv_norm, w_k)
    v_s = _norm_mm(hs, kv_norm, w_v)

    for j in range(n_b):
        w_q = b_w_qg[j][:, :d].astype(BF16)
        w_g = b_w_qg[j][:, d:].astype(BF16)
        w_o = b_w_o[j].astype(BF16)
        gain = final_norm if j == n_b - 1 else None
        qp = _norm_mm(hp, b_norm[j], w_q)
        gp = _norm_mm(hp, b_norm[j], w_g)
        op = _sb_prompt(qp, k_p, v_p, gp, b_logit_bias[j], tp)
        hp = _mm_res(op, w_o, hp, gain)
        qs = _norm_mm(hs, b_norm[j], w_q)
        gs = _norm_mm(hs, b_norm[j], w_g)
        os_ = _sb_sample(qs, k_s, v_s, gs, cache_k, cache_v, page_table, b_logit_bias[j], ts)
        hs = _mm_res(os_, w_o, hs, gain)

    if n_b == 0:
        raise NotImplementedError("trunk without stick-breaking layers")

    return (hp.reshape(nbp, tp, d), hs.reshape(nbs, ts, d),
            jnp.stack(shift_p), jnp.stack(wkv_p),
            k_p.reshape(nbp, tp, nh_b, HEAD_B), v_p.reshape(nbp, tp, nh_b, HEAD_B),
            jnp.stack(shift_s), jnp.stack(wkv_s),
            k_s.reshape(nbs, ts, nh_b, HEAD_B), v_s.reshape(nbs, ts, nh_b, HEAD_B))
```

```python
import functools
import math

import jax
import jax.numpy as jnp
from jax import lax
from jax.experimental import pallas as pl
from jax.experimental.pallas import tpu as pltpu

F32 = jnp.float32
BF16 = jnp.bfloat16

HEAD_A = 64
HEAD_A_SHIFT = 6
HEAD_B = 128
LANES = 128
SUBLANES = 8
RMS_EPS = 1e-6
LNX_EPS = 64e-5
LORA_PAD = 128
WKV_CHUNK = 64
WKV_PAIRS = 16
WKV_P_MX = 1
WKV_P_INV = 1
WKV_P_APPLY = 1
WKV_P_STATE = 2
VMEM_LIMIT = 56 * 1024 * 1024

NN = (((1,), (0,)), ((), ()))
NT = (((1,), (1,)), ((), ()))
TN = (((0,), (0,)), ((), ()))


def _split(x, n):
    parts = []
    for _ in range(n - 1):
        hi = x.astype(BF16)
        parts.append(hi)
        x = x - hi.astype(F32)
    parts.append(x.astype(BF16))
    return parts


def _dot(a, b, dims=NN, pa=1, pb=1):
    ap = _split(a, pa) if pa > 1 or a.dtype != BF16 else [a]
    bp = _split(b, pb) if pb > 1 or b.dtype != BF16 else [b]
    n = max(pa, pb)
    acc = None
    for i in reversed(range(len(ap))):
        for j in reversed(range(len(bp))):
            if i + j < n:
                t = lax.dot_general(ap[i], bp[j], dims, preferred_element_type=F32)
                acc = t if acc is None else acc + t
    return acc


def _dotp(a, b, dims, level):
    return _dot(a, b, dims, pa=level, pb=level)


def _softplus(z):
    return jnp.maximum(z, 0.0) + jnp.log(1.0 + jnp.exp(-jnp.abs(z)))


def _silu(g):
    return g / (1.0 + jnp.exp(-g))


def _rms(x, gain):
    return x * lax.rsqrt(jnp.mean(x * x, axis=-1, keepdims=True) + RMS_EPS) * gain


def _params(*sem):
    return pltpu.CompilerParams(dimension_semantics=sem, vmem_limit_bytes=VMEM_LIMIT)


def _nsl_kernel(x_ref, xh_ref, sh_ref, gain_ref, mu_ref, w1_ref, w2_ref, w0_ref,
                a1_ref, a2_ref, a0_ref, xs_ref, h_ref, lw_ref, a_ref, *, bm, seq, per_row_start):
    i = pl.program_id(0)
    gain = gain_ref[...]
    h = _rms(x_ref[...], gain)
    rolled = pltpu.roll(h, 1, 0)
    row = lax.broadcasted_iota(jnp.int32, (bm, 1), 0)
    if per_row_start:
        h_prev = jnp.where((row & (seq - 1)) == 0, sh_ref[...], rolled)
    else:
        halo = _rms(xh_ref[0], gain)
        first = jnp.where((i * bm) % seq == 0, sh_ref[...], halo[SUBLANES - 1:SUBLANES])
        h_prev = jnp.where(row == 0, first, rolled)
    dx = h_prev - h
    h_ref[...] = h
    for p in range(4):
        xs_ref[p] = (h + dx * mu_ref[p:p + 1, :]).astype(BF16)
    x4 = (h + dx * mu_ref[4:5, :]).astype(BF16)
    t = jnp.tanh(jnp.dot(x4, w1_ref[...], preferred_element_type=F32)).astype(BF16)
    wl = jnp.dot(t, w2_ref[...], preferred_element_type=F32) + w0_ref[...]
    w_raw = jnp.minimum(wl, 0.0) - jnp.log(1.0 + jnp.exp(-jnp.abs(wl))) - 0.5
    lw_ref[...] = -jnp.exp(w_raw)
    x5 = (h + dx * mu_ref[5:6, :]).astype(BF16)
    u = jnp.dot(x5, a1_ref[...], preferred_element_type=F32).astype(BF16)
    al = jnp.dot(u, a2_ref[...], preferred_element_type=F32) + a0_ref[...]
    a_ref[...] = 1.0 / (1.0 + jnp.exp(-al))


def _norm_shift_lora(x, shift, seq, gain, mu, w1, w2, w0, a1, a2, a0):
    m, d = x.shape
    per_row_start = seq < SUBLANES * 2
    bm = m if per_row_start else min(256, seq)
    assert m % bm == 0 and (per_row_start or seq % bm == 0)
    nb = m // seq
    if per_row_start:
        sh = jnp.repeat(shift, seq, axis=0)
        sh_spec = pl.BlockSpec((bm, d), lambda i: (i, 0))
    else:
        sh = shift.reshape(nb, 1, d)
        sh_spec = pl.BlockSpec((None, 1, d), lambda i: ((i * bm) // seq, 0, 0))
    xh = x.reshape(m // SUBLANES, SUBLANES, d)
    row = lambda i: (i, 0)
    const = lambda i: (0, 0)
    kern = functools.partial(_nsl_kernel, bm=bm, seq=seq, per_row_start=per_row_start)
    return pl.pallas_call(
        kern,
        grid=(m // bm,),
        in_specs=[
            pl.BlockSpec((bm, d), row),
            pl.BlockSpec((1, SUBLANES, d), lambda i: (jnp.maximum(i * (bm // SUBLANES) - 1, 0), 0, 0)),
            sh_spec,
            pl.BlockSpec((1, d), const),
            pl.BlockSpec(mu.shape, const),
            pl.BlockSpec(w1.shape, const),
            pl.BlockSpec(w2.shape, const),
            pl.BlockSpec((1, d), const),
            pl.BlockSpec(a1.shape, const),
            pl.BlockSpec(a2.shape, const),
            pl.BlockSpec((1, d), const),
        ],
        out_specs=[
            pl.BlockSpec((4, bm, d), lambda i: (0, i, 0)),
            pl.BlockSpec((bm, d), row),
            pl.BlockSpec((bm, d), row),
            pl.BlockSpec((bm, d), row),
        ],
        out_shape=[
            jax.ShapeDtypeStruct((4, m, d), BF16),
            jax.ShapeDtypeStruct((m, d), F32),
            jax.ShapeDtypeStruct((m, d), F32),
            jax.ShapeDtypeStruct((m, d), F32),
        ],
        compiler_params=_params("arbitrary"),
        name="norm_shift_lora",
    )(x, xh, sh, gain.reshape(1, d), mu, w1, w2, w0.reshape(1, d), a1, a2, a0.reshape(1, d))


def _bmm_kernel(x_ref, w_ref, o_ref):
    o_ref[...] = jnp.dot(x_ref[...], w_ref[...], preferred_element_type=F32)


def _bmm(x, w):
    p, m, k = x.shape
    n = w.shape[2]
    bm = min(512, m)
    assert m % bm == 0
    return pl.pallas_call(
        _bmm_kernel,
        grid=(p, m // bm),
        in_specs=[
            pl.BlockSpec((None, bm, k), lambda q, i: (q, i, 0)),
            pl.BlockSpec((None, k, n), lambda q, i: (q, 0, 0)),
        ],
        out_specs=pl.BlockSpec((None, bm, n), lambda q, i: (q, i, 0)),
        out_shape=jax.ShapeDtypeStruct((p, m, n), F32),
        compiler_params=_params("arbitrary", "arbitrary"),
        name="bmm",
    )(x, w)


def _wkv_kernel(r_ref, k_ref, v_ref, g_ref, w_ref, a_ref, kk_ref, ka_ref, rk_ref, lg_ref, lb_ref,
                s0_ref, o_ref, so_ref, y_ref, *, chunk, rows, pairs):
    c = pl.program_id(2)
    nc = pl.num_programs(2)
    C = chunk
    C2 = 2 * C

    @pl.when(c == 0)
    def _():
        y_ref[...] = s0_ref[...]

    def load(ref, ln):
        x = ref[:, ln]
        if rows < C:
            x = jnp.concatenate([x, jnp.zeros((C - rows, LANES), F32)], axis=0)
        return x

    lane = lax.broadcasted_iota(jnp.int32, (1, LANES), 1)
    head0 = lane < HEAD_A
    ri = lax.broadcasted_iota(jnp.int32, (C2, C2), 0)
    ci = lax.broadcasted_iota(jnp.int32, (C2, C2), 1)
    same_head = (ri >> HEAD_A_SHIFT) == (ci >> HEAD_A_SHIFT)
    si = ri & (C - 1)
    sj = ci & (C - 1)
    strict = sj < si
    incl = sj <= si
    eye = jnp.where(ri == ci, 1.0, 0.0)
    ti = lax.broadcasted_iota(jnp.int32, (C, C), 0)
    tj = lax.broadcasted_iota(jnp.int32, (C, C), 1)
    tril = jnp.where(tj <= ti, 1.0, 0.0).astype(BF16)

    def stack(x):
        return jnp.concatenate([jnp.where(head0, x, 0.0), jnp.where(head0, 0.0, x)], axis=0)

    def pack(x):
        return x[:C] + x[C:]

    def head_sum(x):
        s0 = jnp.sum(jnp.where(head0, x, 0.0), axis=1, keepdims=True)
        s1 = jnp.sum(jnp.where(head0, 0.0, x), axis=1, keepdims=True)
        return jnp.where(head0, s0, s1)

    def pair_chain(p):
        ln = slice(p * LANES, (p + 1) * LANES)
        R, K, V, Wl, A = (load(ref, ln) for ref in (r_ref, k_ref, v_ref, w_ref, a_ref))

        kkp = K * kk_ref[:, ln]
        kk = kkp / jnp.maximum(jnp.sqrt(head_sum(kkp * kkp)), 1e-12)
        Km = K * (1.0 + (A - 1.0) * ka_ref[:, ln])

        Lc = _dot(tril, Wl, pb=3)
        yield
        Lend = Lc[C - 1:C, :]
        g_inv = jnp.exp(-Lc)
        g_rem = jnp.exp(Lend - Lc)
        kka = kk * A
        Ah = -kk * jnp.exp(Lc - Wl)
        Rh = R * jnp.exp(Lc)
        Bc = kka * g_inv
        Kc = Km * g_inv
        Bt = kka * g_rem
        Kt = Km * g_rem

        sAh = stack(Ah)
        lhs = jnp.concatenate([sAh, stack(Rh)], axis=0)
        rhs = jnp.concatenate([stack(Bc), stack(Kc)], axis=0)
        mx = _dotp(lhs, rhs, NT, WKV_P_MX)
        yield
        Nab =jnp.where(strict, mx[:C2, :C2], 0.0)
        nrest = jnp.concatenate([jnp.where(strict, mx[:C2, C2:], 0.0),
                                 jnp.where(incl, mx[C2:, C2:], 0.0)], axis=0)
        Nrb = jnp.where(incl, mx[C2:, :C2], 0.0)

        Tm = eye + Nab
        Pw = Nab
        nv = _dotp(nrest, stack(V), NN, WKV_P_APPLY)
        yield
        for _ in range(int(math.log2(C)) - 1):
            Pw = _dotp(Pw, Pw, NN, WKV_P_INV)
            yield
            Tm = Tm + _dotp(Tm, Pw, NN, WKV_P_INV)
            yield
        if WKV_P_INV == 1:
            resid = (eye - Tm) + _dotp(Nab, Tm, NN, 2)
            yield
            Tm = Tm + _dotp(Tm, resid, NN, 1)
            yield

        tx = _dotp(Tm, jnp.concatenate([sAh, nv[:C2]], axis=1), NN, WKV_P_APPLY)
        yield

        Y = y_ref[p]
        sy = _dotp(jnp.concatenate([tx[:, :LANES], Rh], axis=0), Y, NT, WKV_P_STATE)
        yield
        Ub = sy[:C2] + tx[:, LANES:]
        U = pack(Ub)
        O = sy[C2:] + pack(_dotp(Nrb, Ub, NN, WKV_P_STATE) + nv[C2:])
        upd = _dotp(jnp.concatenate([U, V], axis=0), jnp.concatenate([Bt, Kt], axis=0), TN, WKV_P_STATE)
        yield
        y_new = Y * jnp.exp(Lend) + jnp.where(same_head, upd, 0.0)

        mean = head_sum(O) * (1.0 / HEAD_A)
        dlt = O - mean
        var = head_sum(dlt * dlt) * (1.0 / HEAD_A)
        on = dlt * lax.rsqrt(var + LNX_EPS) * lg_ref[:, ln] + lb_ref[:, ln]
        bonus = head_sum(R * Km * rk_ref[:, ln]) * V
        res = (on + bonus) * _silu(load(g_ref, ln))
        yield
        y_ref[p] = y_new
        o_ref[:, ln] = res[:rows].astype(o_ref.dtype)

    chains = [pair_chain(p) for p in range(pairs)]
    while chains:
        chains = [ch for ch in chains if next(ch, True) is None]

    @pl.when(c == nc - 1)
    def _():
        so_ref[...] = y_ref[...]


def _wkv(rkvg, lw, a, s0_bd, seq, k_k, k_a, r_k, lnx_g, lnx_b):
    _, m, d = rkvg.shape
    nb = m // seq
    npair = d // LANES
    pairs = math.gcd(npair, WKV_PAIRS)
    width = pairs * LANES
    rows = min(WKV_CHUNK, seq)
    assert seq % rows == 0 and rows % SUBLANES == 0
    nc = seq // rows
    tok = lambda b, h, c: (b * nc + c, h)
    par = lambda b, h, c: (0, h)
    st = lambda b, h, c: (b, h, 0, 0)

    def proj(p):
        return pl.BlockSpec((None, rows, width), lambda b, h, c: (p, b * nc + c, h))

    vec = lambda x: x.reshape(1, d)
    kern = functools.partial(_wkv_kernel, chunk=WKV_CHUNK, rows=rows, pairs=pairs)
    return pl.pallas_call(
        kern,
        grid=(nb, npair // pairs, nc),
        in_specs=[proj(0), proj(1), proj(2), proj(3),
                  pl.BlockSpec((rows, width), tok), pl.BlockSpec((rows, width), tok),
                  pl.BlockSpec((1, width), par), pl.BlockSpec((1, width), par),
                  pl.BlockSpec((1, width), par), pl.BlockSpec((1, width), par),
                  pl.BlockSpec((1, width), par),
                  pl.BlockSpec((None, pairs, LANES, LANES), st)],
        out_specs=[pl.BlockSpec((rows, width), tok),
                   pl.BlockSpec((None, pairs, LANES, LANES), st)],
        out_shape=[jax.ShapeDtypeStruct((m, d), BF16 if rows % (2 * SUBLANES) == 0 else F32),
                   jax.ShapeDtypeStruct((nb, npair, LANES, LANES), F32)],
        scratch_shapes=[pltpu.VMEM((pairs, LANES, LANES), F32)],
        compiler_params=_params("arbitrary", "arbitrary", "arbitrary"),
        name="wkv",
    )(rkvg, rkvg, rkvg, rkvg, lw, a, vec(k_k), vec(k_a), vec(r_k), vec(lnx_g), vec(lnx_b), s0_bd)


def _state_to_pairs(s):
    nb, nh, n, _ = s.shape
    s = s.reshape(nb, nh // 2, 2, n, n)
    z = jnp.zeros_like(s[:, :, 0])
    top = jnp.concatenate([s[:, :, 0], z], axis=-1)
    bot = jnp.concatenate([z, s[:, :, 1]], axis=-1)
    return jnp.concatenate([top, bot], axis=-2)


def _pairs_to_state(y):
    nb, npair = y.shape[:2]
    n = HEAD_A
    s = jnp.stack([y[:, :, :n, :n], y[:, :, n:, n:]], axis=2)
    return s.reshape(nb, 2 * npair, n, n)


def _mm_res_kernel(x_ref, w_ref, r_ref, g_ref, o_ref, *, final_norm):
    y = r_ref[...] + jnp.dot(x_ref[...].astype(BF16), w_ref[...], preferred_element_type=F32)
    if final_norm:
        y = _rms(y, g_ref[...])
    o_ref[...] = y


def _mm_res(x, w, res, gain=None):
    m, k = x.shape
    n = w.shape[1]
    bm = min(256, m)
    assert m % bm == 0
    final_norm = gain is not None
    g = (gain if final_norm else jnp.ones((n,), F32)).reshape(1, n)
    return pl.pallas_call(
        functools.partial(_mm_res_kernel, final_norm=final_norm),
        grid=(m // bm,),
        in_specs=[pl.BlockSpec((bm, k), lambda i: (i, 0)),
                  pl.BlockSpec((k, n), lambda i: (0, 0)),
                  pl.BlockSpec((bm, n), lambda i: (i, 0)),
                  pl.BlockSpec((1, n), lambda i: (0, 0))],
        out_specs=pl.BlockSpec((bm, n), lambda i: (i, 0)),
        out_shape=jax.ShapeDtypeStruct((m, n), F32),
        compiler_params=_params("arbitrary"),
        name="mm_res",
    )(x, w, res, g)


def _norm_mm_kernel(x_ref, g_ref, w_ref, o_ref):
    xn = _rms(x_ref[...], g_ref[...]).astype(BF16)
    o_ref[...] = jnp.dot(xn, w_ref[...], preferred_element_type=F32)


def _norm_mm(x, gain, w):
    m, k = x.shape
    n = w.shape[1]
    bm = min(256, m)
    assert m % bm == 0
    return pl.pallas_call(
        _norm_mm_kernel,
        grid=(m // bm,),
        in_specs=[pl.BlockSpec((bm, k), lambda i: (i, 0)),
                  pl.BlockSpec((1, k), lambda i: (0, 0)),
                  pl.BlockSpec((k, n), lambda i: (0, 0))],
        out_specs=pl.BlockSpec((bm, n), lambda i: (i, 0)),
        out_shape=jax.ShapeDtypeStruct((m, n), F32),
        compiler_params=_params("arbitrary"),
        name="norm_mm",
    )(x, gain.reshape(1, k), w)


SB_BLOCK = 256
SB_HEADS = 4
SB_CUM_PIECES = 2
SBS_PAGES = 4


def _sbp_kernel(bias_ref, q_ref, k_ref, v_ref, g_ref, o_ref, kb_ref, vb_ref, *, blk, scale, heads):
    hg = pl.program_id(1)
    qi = pl.program_id(2)

    @pl.when(qi == 0)
    def _():
        kb_ref[...] = k_ref[...].astype(BF16)
        vb_ref[...] = v_ref[...].astype(BF16)

    lanes = [slice(h * HEAD_B, (h + 1) * HEAD_B) for h in range(heads)]
    biases = [bias_ref[hg * heads + h] for h in range(heads)]
    qs = [(q_ref[:, ln] * scale).astype(BF16) for ln in lanes]
    ri = lax.broadcasted_iota(jnp.int32, (blk, blk), 0)
    ci = lax.broadcasted_iota(jnp.int32, (blk, blk), 1)
    suffix = jnp.where(ri > ci, 1.0, 0.0).astype(BF16)
    causal = ci < ri

    def block(j, carries, accs, diag):
        rows = pl.ds(pl.multiple_of(j * blk, blk), blk)
        zs = [lax.dot_general(q, kb_ref[rows, ln], NT, preferred_element_type=F32) + b
              for q, ln, b in zip(qs, lanes, biases)]
        sps = [_softplus(z) for z in zs]
        lks = [jnp.where(causal, -sp, 0.0) if diag else -sp for sp in sps]
        rests = [_dot(lk, suffix, pa=SB_CUM_PIECES) for lk in lks]
        atts = [jnp.exp(z - sp + rs + c) for z, sp, rs, c in zip(zs, sps, rests, carries)]
        if diag:
            atts = [jnp.where(causal, att, 0.0) for att in atts]
        accs = [acc + jnp.dot(att.astype(BF16), vb_ref[rows, ln], preferred_element_type=F32)
                for acc, att, ln in zip(accs, atts, lanes)]
        carries = [c + jnp.sum(lk, axis=1, keepdims=True) for c, lk in zip(carries, lks)]
        return carries, accs

    carries = [jnp.zeros((blk, 1), F32)] * heads
    accs = [jnp.zeros((blk, HEAD_B), F32)] * heads
    carries, accs = block(qi, carries, accs, True)

    def body(it, ca):
        cs, as_ = block(qi - 1 - it, list(ca[:heads]), list(ca[heads:]), False)
        return tuple(cs) + tuple(as_)

    ca = lax.fori_loop(0, qi, body, tuple(carries) + tuple(accs))
    for h, ln in enumerate(lanes):
        o_ref[:, ln] = (ca[heads + h] * _silu(g_ref[:, ln])).astype(o_ref.dtype)


def _sb_prompt(q, k, v, g, bias, seq):
    m, d = q.shape
    nb = m // seq
    nh = d // HEAD_B
    heads = math.gcd(nh, SB_HEADS)
    width = heads * HEAD_B
    blk = min(SB_BLOCK, seq)
    assert seq % blk == 0
    nq = seq // blk
    qmap = lambda b, h, i: (b * nq + i, h)
    kmap = lambda b, h, i: (b, h)
    kern = functools.partial(_sbp_kernel, blk=blk, scale=1.0 / math.sqrt(HEAD_B), heads=heads)
    return pl.pallas_call(
        kern,
        grid=(nb, nh // heads, nq),
        in_specs=[pl.BlockSpec(memory_space=pltpu.SMEM),
                  pl.BlockSpec((blk, width), qmap),
                  pl.BlockSpec((seq, width), kmap),
                  pl.BlockSpec((seq, width), kmap),
                  pl.BlockSpec((blk, width), qmap)],
        out_specs=pl.BlockSpec((blk, width), qmap),
        out_shape=jax.ShapeDtypeStruct((m, d), BF16),
        scratch_shapes=[pltpu.VMEM((seq, width), BF16), pltpu.VMEM((seq, width), BF16)],
        compiler_params=_params("arbitrary", "arbitrary", "arbitrary"),
        name="sb_prompt",
    )(bias, q, k, v, g)


def _wide_page(refs, page):
    ht = refs[0].shape[1]
    rows = [r.reshape(page * ht, HEAD_B) for r in refs]
    return jnp.concatenate([r[pl.ds(j, page, stride=ht), :].astype(BF16) for r in rows for j in range(ht)],
                           axis=1)


def _sbs_kernel(pt_ref, qbd_ref, bias_ref, kn_ref, vn_ref, g_ref, *rest, ts, page, nh, nkv):
    del pt_ref
    k_refs, v_refs = rest[:nkv], rest[nkv:2 * nkv]
    o_ref, acc_ref, carry_ref = rest[2 * nkv:]
    ntile = nh // k_refs[0].shape[1]
    p = pl.program_id(1)
    last = pl.num_programs(1) - 1
    cols = nh * ts
    qbd = qbd_ref[...]
    bias = bias_ref[...]
    ri = lax.broadcasted_iota(jnp.int32, (page, page), 0)
    ci = lax.broadcasted_iota(jnp.int32, (page, page), 1)
    suffix = jnp.where(ci > ri, 1.0, 0.0).astype(BF16)

    wide = functools.partial(_wide_page, page=page)

    def process(ks, vs, mask):
        zs = [jnp.dot(k, qbd, preferred_element_type=F32) + bias for k in ks]
        sps = [_softplus(z) for z in zs]
        lks = [-sp if mask is None else jnp.where(mask, -sp, 0.0) for sp in sps]
        rests = [_dot(suffix, lk, pb=SB_CUM_PIECES) for lk in lks]
        carry = carry_ref[...]
        atts = []
        for z, sp, lk, rs in zip(zs, sps, lks, rests):
            att = jnp.exp(z - sp + rs + carry)
            if mask is not None:
                att = jnp.where(mask, att, 0.0)
            atts.append(att.astype(BF16))
            carry = carry + jnp.sum(lk, axis=0, keepdims=True)
        att = atts[0] if len(atts) == 1 else jnp.concatenate(atts, axis=0)
        val = vs[0] if len(vs) == 1 else jnp.concatenate(vs, axis=0)
        acc_ref[...] += lax.dot_general(att, val, TN, preferred_element_type=F32)
        carry_ref[...] = carry

    @pl.when(p == 0)
    def _():
        acc_ref[...] = jnp.zeros_like(acc_ref)
        carry_ref[...] = jnp.zeros_like(carry_ref)
        pad = jnp.zeros((page - ts, kn_ref.shape[-1]), BF16)
        kn = jnp.concatenate([kn_ref[...].astype(BF16), pad], axis=0)
        vn = jnp.concatenate([vn_ref[...].astype(BF16), pad], axis=0)
        krow = lax.broadcasted_iota(jnp.int32, (page, cols), 0)
        qcol = lax.broadcasted_iota(jnp.int32, (page, cols), 1) & (ts - 1)
        process([kn], [vn], krow < qcol)

    @pl.when(p > 0)
    def _():
        pages = range(0, nkv, ntile)
        process([wide(k_refs[j:j + ntile]) for j in pages], [wide(v_refs[j:j + ntile]) for j in pages], None)

    @pl.when(p == last)
    def _():
        g = g_ref[...]
        for h in range(nh):
            sl = slice(h * HEAD_B, (h + 1) * HEAD_B)
            o = acc_ref[h * ts:(h + 1) * ts, sl]
            o_ref[:, sl] = (o * _silu(g[:, sl])).astype(o_ref.dtype)


def _sb_sample(q, k_new, v_new, g, cache_k, cache_v, page_table, bias, ts):
    m, d = q.shape
    nb = m // ts
    nh = d // HEAD_B
    n_pool, page = cache_k.shape[:2]
    n_pages = page_table.shape[1]
    cols = nh * ts
    assert ts % SUBLANES == 0 and ts & (ts - 1) == 0
    scale = 1.0 / math.sqrt(HEAD_B)
    q4 = (q * scale).reshape(nb, ts, nh, HEAD_B)
    eye = jnp.eye(nh, dtype=F32)
    qbd = jnp.einsum("bihc,hg->bhcgi", q4, eye).reshape(nb, d, cols).astype(BF16)
    bias_cols = jnp.repeat(bias, ts).reshape(1, cols)
    ht = math.gcd(nh, SUBLANES)
    ntile = nh // ht
    npg = math.gcd(n_pages, SBS_PAGES)
    nkv = npg * ntile
    tok = lambda b, p, pt: (b, 0, 0)

    def page_spec(j, t):
        return pl.BlockSpec((None, page, ht, HEAD_B),
                            lambda b, p, pt: (pt[b, n_pages - 1 - (jnp.maximum(p, 1) - 1) * npg - j], 0, t, 0))

    page_specs = [page_spec(j, t) for j in range(npg) for t in range(ntile)]
    grid_spec = pltpu.PrefetchScalarGridSpec(
        num_scalar_prefetch=1,
        grid=(nb, n_pages // npg + 1),
        in_specs=[pl.BlockSpec((None, d, cols), tok),
                  pl.BlockSpec((1, cols), lambda b, p, pt: (0, 0)),
                  pl.BlockSpec((None, ts, d), tok),
                  pl.BlockSpec((None, ts, d), tok),
                  pl.BlockSpec((None, ts, d), tok)] + page_specs + page_specs,
        out_specs=pl.BlockSpec((None, ts, d), tok),
        scratch_shapes=[pltpu.VMEM((cols, d), F32), pltpu.VMEM((1, cols), F32)],
    )
    kern = functools.partial(_sbs_kernel, ts=ts, page=page, nh=nh, nkv=nkv)
    out = pl.pallas_call(
        kern,
        grid_spec=grid_spec,
        out_shape=jax.ShapeDtypeStruct((nb, ts, d), F32),
        compiler_params=_params("arbitrary", "arbitrary"),
        name="sb_sample",
    )(page_table, qbd, bias_cols, k_new.reshape(nb, ts, d), v_new.reshape(nb, ts, d), g.reshape(nb, ts, d),
      *([cache_k] * nkv), *([cache_v] * nkv))
    return out.reshape(m, d)


def _pad_lora(w_in, w_out):
    r = w_in.shape[1]
    w_in = jnp.pad(w_in, ((0, 0), (0, LORA_PAD - r))).astype(BF16)
    w_out = jnp.pad(w_out, ((0, LORA_PAD - r), (0, 0))).astype(BF16)
    return w_in, w_out


def kernel(x_prompt, x_sample, state_shift, state_wkv, cache_k, cache_v, page_table, a_norm, a_mu, a_w_rkvg, a_w0, a_w1, a_w2, a_a0, a_a1, a_a2, a_k_k, a_k_a, a_r_k, a_lnx_g, a_lnx_b, a_w_o, kv_norm, w_kv, b_norm, b_w_qg, b_logit_bias, b_w_o, final_norm):
    nbp, tp, d = x_prompt.shape
    nbs, ts, _ = x_sample.shape
    n_a = a_norm.shape[0]
    n_b = b_norm.shape[0]
    nh_a = d // HEAD_A
    nh_b = d // HEAD_B

    hp = x_prompt.reshape(nbp * tp, d)
    hs = x_sample.reshape(nbs * ts, d)
    shift_p, wkv_p, shift_s, wkv_s = [], [], [], []

    for i in range(n_a):
        w1, w2 = _pad_lora(a_w1[i], a_w2[i])
        a1, a2 = _pad_lora(a_a1[i], a_a2[i])
        w_rkvg = a_w_rkvg[i].astype(BF16)
        w_o = a_w_o[i].astype(BF16)
        r_k = a_r_k[i].reshape(d)

        def a_layer(h_in, seq, shift0, s0):
            nb = h_in.shape[0] // seq
            xs, hn, lw, a = _norm_shift_lora(h_in, shift0, seq, a_norm[i], a_mu[i], w1, w2, a_w0[i],
                                             a1, a2, a_a0[i])
            rkvg = _bmm(xs, w_rkvg)
            og, y = _wkv(rkvg, lw, a, _state_to_pairs(s0.astype(F32)), seq,
                         a_k_k[i], a_k_a[i], r_k, a_lnx_g[i], a_lnx_b[i])
            h_out = _mm_res(og, w_o, h_in)
            return h_out, hn.reshape(nb, seq, d)[:, -1], _pairs_to_state(y).astype(s0.dtype)

        hp, sh, st = a_layer(hp, tp, jnp.zeros((nbp, d), F32),
                             jnp.zeros((nbp,) + state_wkv.shape[2:], state_wkv.dtype))
        shift_p.append(sh); wkv_p.append(st)
        hs, sh, st = a_layer(hs, ts, state_shift[i], state_wkv[i])
        shift_s.append(sh); wkv_s.append(st)

    w_k = w_kv[:, :d].astype(BF16)
    w_v = w_kv[:, d:].astype(BF16)
    k_p = _norm_mm(hp, kv_norm, w_k)
    v_p = _norm_mm(hp, kv_norm, w_v)
    k_s = _norm_mm(hs, kv_norm, w_k)
    v_s = _norm_mm(hs, kv_norm, w_v)

    for j in range(n_b):
        w_q = b_w_qg[j][:, :d].astype(BF16)
        w_g = b_w_qg[j][:, d:].astype(BF16)
        w_o = b_w_o[j].astype(BF16)
        gain = final_norm if j == n_b - 1 else None
        qp = _norm_mm(hp, b_norm[j], w_q)
        gp = _norm_mm(hp, b_norm[j], w_g)
        op = _sb_prompt(qp, k_p, v_p, gp, b_logit_bias[j], tp)
        hp = _mm_res(op, w_o, hp, gain)
        qs = _norm_mm(hs, b_norm[j], w_q)
        gs = _norm_mm(hs, b_norm[j], w_g)
        os_ = _sb_sample(qs, k_s, v_s, gs, cache_k, cache_v, page_table, b_logit_bias[j], ts)
        hs = _mm_res(os_, w_o, hs, gain)

    if n_b == 0:
        raise NotImplementedError("trunk without stick-breaking layers")

    return (hp.reshape(nbp, tp, d), hs.reshape(nbs, ts, d),
            jnp.stack(shift_p), jnp.stack(wkv_p),
            k_p.reshape(nbp, tp, nh_b, HEAD_B), v_p.reshape(nbp, tp, nh_b, HEAD_B),
            jnp.stack(shift_s), jnp.stack(wkv_s),
            k_s.reshape(nbs, ts, nh_b, HEAD_B), v_s.reshape(nbs, ts, nh_b, HEAD_B))
```

```python
import functools
import math

import jax
import jax.numpy as jnp
from jax import lax
from jax.experimental import pallas as pl
from jax.experimental.pallas import tpu as pltpu

F32 = jnp.float32
BF16 = jnp.bfloat16

HEAD_A = 64
HEAD_A_SHIFT = 6
HEAD_B = 128
LANES = 128
SUBLANES = 8
LOG2E = 1.4426950408889634
RMS_EPS = 1e-6
LNX_EPS = 64e-5
LORA_PAD = 128
WKV_CHUNK = 64
WKV_PAIRS = 16
WKV_P_MX = 1
WKV_P_INV = 1
WKV_P_APPLY = 1
WKV_P_STATE = 1
VMEM_LIMIT = 56 * 1024 * 1024
ROW_TILE = 512

NN = (((1,), (0,)), ((), ()))
NT = (((1,), (1,)), ((), ()))
TN = (((0,), (0,)), ((), ()))


def _split(x, n):
    parts = []
    for _ in range(n - 1):
        hi = x.astype(BF16)
        parts.append(hi)
        x = x - hi.astype(F32)
    parts.append(x.astype(BF16))
    return parts


def _dot(a, b, dims=NN, pa=1, pb=1):
    ap = _split(a, pa) if pa > 1 or a.dtype != BF16 else [a]
    bp = _split(b, pb) if pb > 1 or b.dtype != BF16 else [b]
    n = max(pa, pb)
    acc = None
    for j in reversed(range(len(bp))):
        lhs = [ap[i] for i in reversed(range(len(ap))) if i + j < n]
        if dims == TN or len(lhs) == 1:
            terms = [lax.dot_general(x, bp[j], dims, preferred_element_type=F32) for x in lhs]
        else:
            m = a.shape[0]
            t = lax.dot_general(jnp.concatenate(lhs, axis=0), bp[j], dims, preferred_element_type=F32)
            terms = [t[k * m:(k + 1) * m] for k in range(len(lhs))]
        for t in terms:
            acc = t if acc is None else acc + t
    return acc


def _dotp(a, b, dims, level):
    return _dot(a, b, dims, pa=level, pb=level)


def _softplus2(z2):
    return jnp.maximum(z2, 0.0) + jnp.log(1.0 + jnp.exp2(-jnp.abs(z2))) * LOG2E


def _silu(g):
    return g / (1.0 + jnp.exp(-g))


def _rms(x, gain):
    return x * lax.rsqrt(jnp.mean(x * x, axis=-1, keepdims=True) + RMS_EPS) * gain


def _params(*sem):
    return pltpu.CompilerParams(dimension_semantics=sem, vmem_limit_bytes=VMEM_LIMIT)


def _nsl_kernel(x_ref, xh_ref, sh_ref, gain_ref, mu_ref, w1_ref, w2_ref, w0_ref,
                a1_ref, a2_ref, a0_ref, xs_ref, h_ref, lw_ref, a_ref, *, bm, seq, per_row_start):
    i = pl.program_id(0)
    gain = gain_ref[...]
    h = _rms(x_ref[...], gain)
    rolled = pltpu.roll(h, 1, 0)
    row = lax.broadcasted_iota(jnp.int32, (bm, 1), 0)
    if per_row_start:
        h_prev = jnp.where((row & (seq - 1)) == 0, sh_ref[...], rolled)
    else:
        halo = _rms(xh_ref[0], gain)
        first = jnp.where((i * bm) % seq == 0, sh_ref[...], halo[SUBLANES - 1:SUBLANES])
        h_prev = jnp.where(row == 0, first, rolled)
    dx = h_prev - h
    h_ref[...] = h if per_row_start else h[bm - SUBLANES:]
    for p in range(4):
        xs_ref[p] = (h + dx * mu_ref[p:p + 1, :]).astype(BF16)
    x4 = (h + dx * mu_ref[4:5, :]).astype(BF16)
    t = jnp.tanh(jnp.dot(x4, w1_ref[...], preferred_element_type=F32)).astype(BF16)
    wl = jnp.dot(t, w2_ref[...], preferred_element_type=F32) + w0_ref[...]
    w_raw = jnp.minimum(wl, 0.0) - jnp.log(1.0 + jnp.exp(-jnp.abs(wl))) - 0.5
    lw_ref[...] = -jnp.exp(w_raw)
    x5 = (h + dx * mu_ref[5:6, :]).astype(BF16)
    u = jnp.dot(x5, a1_ref[...], preferred_element_type=F32).astype(BF16)
    al = jnp.dot(u, a2_ref[...], preferred_element_type=F32) + a0_ref[...]
    a_ref[...] = 1.0 / (1.0 + jnp.exp(-al))


def _norm_shift_lora(x, shift, seq, gain, mu, w1, w2, w0, a1, a2, a0):
    m, d = x.shape
    per_row_start = seq < SUBLANES * 2
    bm = m if per_row_start else min(256, seq)
    assert m % bm == 0 and (per_row_start or seq % bm == 0)
    nb = m // seq
    if per_row_start:
        sh = jnp.repeat(shift, seq, axis=0)
        sh_spec = pl.BlockSpec((bm, d), lambda i: (i, 0))
    else:
        sh = shift.reshape(nb, 1, d)
        sh_spec = pl.BlockSpec((None, 1, d), lambda i: ((i * bm) // seq, 0, 0))
    xh = x.reshape(m // SUBLANES, SUBLANES, d)
    row = lambda i: (i, 0)
    const = lambda i: (0, 0)
    kern = functools.partial(_nsl_kernel, bm=bm, seq=seq, per_row_start=per_row_start)
    if per_row_start:
        h_spec, h_shape = pl.BlockSpec((bm, d), row), (m, d)
    else:
        h_spec, h_shape = pl.BlockSpec((None, SUBLANES, d), lambda i: (i, 0, 0)), (m // bm, SUBLANES, d)
    xs, h_tail, lw, a = pl.pallas_call(
        kern,
        grid=(m // bm,),
        in_specs=[
            pl.BlockSpec((bm, d), row),
            pl.BlockSpec((1, SUBLANES, d), lambda i: (jnp.maximum(i * (bm // SUBLANES) - 1, 0), 0, 0)),
            sh_spec,
            pl.BlockSpec((1, d), const),
            pl.BlockSpec(mu.shape, const),
            pl.BlockSpec(w1.shape, const),
            pl.BlockSpec(w2.shape, const),
            pl.BlockSpec((1, d), const),
            pl.BlockSpec(a1.shape, const),
            pl.BlockSpec(a2.shape, const),
            pl.BlockSpec((1, d), const),
        ],
        out_specs=[
            pl.BlockSpec((4, bm, d), lambda i: (0, i, 0)),
            h_spec,
            pl.BlockSpec((bm, d), row),
            pl.BlockSpec((bm, d), row),
        ],
        out_shape=[
            jax.ShapeDtypeStruct((4, m, d), BF16),
            jax.ShapeDtypeStruct(h_shape, F32),
            jax.ShapeDtypeStruct((m, d), F32),
            jax.ShapeDtypeStruct((m, d), F32),
        ],
        compiler_params=_params("arbitrary"),
        name="norm_shift_lora",
    )(x, xh, sh, gain.reshape(1, d), mu, w1, w2, w0.reshape(1, d), a1, a2, a0.reshape(1, d))
    if per_row_start:
        last = h_tail.reshape(nb, seq, d)[:, -1]
    else:
        last = h_tail.reshape(nb, seq // bm, SUBLANES, d)[:, -1, -1]
    return xs, last, lw, a


def _bmm_kernel(x_ref, w_ref, o_ref, wb_ref):
    @pl.when(pl.program_id(1) == 0)
    def _():
        wb_ref[...] = w_ref[...].astype(BF16)

    o_ref[...] = jnp.dot(x_ref[...], wb_ref[...], preferred_element_type=F32)


def _bmm(x, w):
    p, m, k = x.shape
    n = w.shape[2]
    bm = min(ROW_TILE, m)
    assert m % bm == 0
    return pl.pallas_call(
        _bmm_kernel,
        grid=(p, m // bm),
        in_specs=[
            pl.BlockSpec((None, bm, k), lambda q, i: (q, i, 0)),
            pl.BlockSpec((None, k, n), lambda q, i: (q, 0, 0)),
        ],
        out_specs=pl.BlockSpec((None, bm, n), lambda q, i: (q, i, 0)),
        out_shape=jax.ShapeDtypeStruct((p, m, n), F32),
        scratch_shapes=[pltpu.VMEM((k, n), BF16)],
        compiler_params=_params("arbitrary", "arbitrary"),
        name="bmm",
    )(x, w)


def _wkv_kernel(r_ref, k_ref, v_ref, g_ref, w_ref, a_ref, kk_ref, ka_ref, rk_ref, lg_ref, lb_ref,
                s0_ref, o_ref, so_ref, y_ref, *, chunk, rows, pairs):
    c = pl.program_id(2)
    nc = pl.num_programs(2)
    C = chunk
    C2 = 2 * C

    @pl.when(c == 0)
    def _():
        y_ref[...] = s0_ref[...]

    def load(ref, ln):
        x = ref[:, ln]
        if rows < C:
            x = jnp.concatenate([x, jnp.zeros((C - rows, LANES), F32)], axis=0)
        return x

    lane = lax.broadcasted_iota(jnp.int32, (1, LANES), 1)
    head0 = lane < HEAD_A
    ri = lax.broadcasted_iota(jnp.int32, (C2, C2), 0)
    ci = lax.broadcasted_iota(jnp.int32, (C2, C2), 1)
    same_head = (ri >> HEAD_A_SHIFT) == (ci >> HEAD_A_SHIFT)
    si = ri & (C - 1)
    sj = ci & (C - 1)
    strict = sj < si
    incl = sj <= si
    eye = jnp.where(ri == ci, 1.0, 0.0)
    ti = lax.broadcasted_iota(jnp.int32, (C, C), 0)
    tj = lax.broadcasted_iota(jnp.int32, (C, C), 1)
    tril = jnp.where(tj <= ti, 1.0, 0.0).astype(BF16)

    def stack(x):
        return jnp.concatenate([jnp.where(head0, x, 0.0), jnp.where(head0, 0.0, x)], axis=0)

    def pack(x):
        return x[:C] + x[C:]

    def head_sum(x):
        s0 = jnp.sum(jnp.where(head0, x, 0.0), axis=1, keepdims=True)
        s1 = jnp.sum(jnp.where(head0, 0.0, x), axis=1, keepdims=True)
        return jnp.where(head0, s0, s1)

    def pair_chain(p):
        ln = slice(p * LANES, (p + 1) * LANES)
        R, K, V, Wl, A = (load(ref, ln) for ref in (r_ref, k_ref, v_ref, w_ref, a_ref))

        kkp = K * kk_ref[:, ln]
        kk = kkp / jnp.maximum(jnp.sqrt(head_sum(kkp * kkp)), 1e-12)
        Km = K * (1.0 + (A - 1.0) * ka_ref[:, ln])

        Lc = _dot(tril, Wl, pb=3)
        yield
        Lend = Lc[C - 1:C, :]
        g_inv = jnp.exp(-Lc)
        g_rem = jnp.exp(Lend - Lc)
        kka = kk * A
        Ah = -kk * jnp.exp(Lc - Wl)
        Rh = R * jnp.exp(Lc)
        Bc = kka * g_inv
        Kc = Km * g_inv
        Bt = kka * g_rem
        Kt = Km * g_rem

        sAh = stack(Ah)
        lhs = jnp.concatenate([sAh, stack(Rh)], axis=0)
        rhs = jnp.concatenate([stack(Bc), stack(Kc)], axis=0)
        mx = _dotp(lhs, rhs, NT, WKV_P_MX)
        yield
        Nab =jnp.where(strict, mx[:C2, :C2], 0.0)
        nrest = jnp.concatenate([jnp.where(strict, mx[:C2, C2:], 0.0),
                                 jnp.where(incl, mx[C2:, C2:], 0.0)], axis=0)
        Nrb = jnp.where(incl, mx[C2:, :C2], 0.0)

        Tm = eye + Nab
        Pw = Nab
        nv = _dotp(nrest, stack(V), NN, WKV_P_APPLY)
        yield
        nsq = int(math.log2(C)) - 1
        Pw = _dotp(Pw, Pw, NN, WKV_P_INV)
        yield
        for it in range(nsq):
            if it < nsq - 1:
                both = _dotp(jnp.concatenate([Pw, Tm], axis=0), Pw, NN, WKV_P_INV)
                Pw, Tm = both[:C2], Tm + both[C2:]
            else:
                Tm = Tm + _dotp(Tm, Pw, NN, WKV_P_INV)
            yield
        if WKV_P_INV == 1:
            resid = (eye - Tm) + _dotp(Nab, Tm, NN, 2)
            yield
            Tm = Tm + _dotp(Tm, resid, NN, 1)
            yield

        tx = _dotp(Tm, jnp.concatenate([sAh, nv[:C2]], axis=1), NN, WKV_P_APPLY)
        yield

        Y = y_ref[p]
        sy = _dotp(jnp.concatenate([tx[:, :LANES], Rh], axis=0), Y, NT, WKV_P_STATE)
        yield
        Ub = sy[:C2] + tx[:, LANES:]
        U = pack(Ub)
        O = sy[C2:] + pack(_dotp(Nrb, Ub, NN, WKV_P_STATE) + nv[C2:])
        upd = _dotp(jnp.concatenate([U, V], axis=0), jnp.concatenate([Bt, Kt], axis=0), TN, WKV_P_STATE)
        yield
        y_new = Y * jnp.exp(Lend) + jnp.where(same_head, upd, 0.0)

        mean = head_sum(O) * (1.0 / HEAD_A)
        dlt = O - mean
        var = head_sum(dlt * dlt) * (1.0 / HEAD_A)
        on = dlt * lax.rsqrt(var + LNX_EPS) * lg_ref[:, ln] + lb_ref[:, ln]
        bonus = head_sum(R * Km * rk_ref[:, ln]) * V
        res = (on + bonus) * _silu(load(g_ref, ln))
        yield
        y_ref[p] = y_new
        o_ref[:, ln] = res[:rows].astype(o_ref.dtype)

    chains = [pair_chain(p) for p in range(pairs)]
    while chains:
        chains = [ch for ch in chains if next(ch, True) is None]

    @pl.when(c == nc - 1)
    def _():
        so_ref[...] = y_ref[...]


def _wkv(rkvg, lw, a, s0_bd, seq, k_k, k_a, r_k, lnx_g, lnx_b):
    _, m, d = rkvg.shape
    nb = m // seq
    npair = d // LANES
    pairs = math.gcd(npair, WKV_PAIRS)
    width = pairs * LANES
    rows = min(WKV_CHUNK, seq)
    assert seq % rows == 0 and rows % SUBLANES == 0
    nc = seq // rows
    tok = lambda b, h, c: (b * nc + c, h)
    par = lambda b, h, c: (0, h)
    st = lambda b, h, c: (b, h, 0, 0)

    def proj(p):
        return pl.BlockSpec((None, rows, width), lambda b, h, c: (p, b * nc + c, h))

    vec = lambda x: x.reshape(1, d)
    kern = functools.partial(_wkv_kernel, chunk=WKV_CHUNK, rows=rows, pairs=pairs)
    return pl.pallas_call(
        kern,
        grid=(nb, npair // pairs, nc),
        in_specs=[proj(0), proj(1), proj(2), proj(3),
                  pl.BlockSpec((rows, width), tok), pl.BlockSpec((rows, width), tok),
                  pl.BlockSpec((1, width), par), pl.BlockSpec((1, width), par),
                  pl.BlockSpec((1, width), par), pl.BlockSpec((1, width), par),
                  pl.BlockSpec((1, width), par),
                  pl.BlockSpec((None, pairs, LANES, LANES), st)],
        out_specs=[pl.BlockSpec((rows, width), tok),
                   pl.BlockSpec((None, pairs, LANES, LANES), st)],
        out_shape=[jax.ShapeDtypeStruct((m, d), BF16 if rows % (2 * SUBLANES) == 0 else F32),
                   jax.ShapeDtypeStruct((nb, npair, LANES, LANES), F32)],
        scratch_shapes=[pltpu.VMEM((pairs, LANES, LANES), F32)],
        compiler_params=_params("arbitrary", "arbitrary", "arbitrary"),
        name="wkv",
    )(rkvg, rkvg, rkvg, rkvg, lw, a, vec(k_k), vec(k_a), vec(r_k), vec(lnx_g), vec(lnx_b), s0_bd)


def _state_to_pairs(s):
    nb, nh, n, _ = s.shape
    s = s.reshape(nb, nh // 2, 2, n, n)
    z = jnp.zeros_like(s[:, :, 0])
    top = jnp.concatenate([s[:, :, 0], z], axis=-1)
    bot = jnp.concatenate([z, s[:, :, 1]], axis=-1)
    return jnp.concatenate([top, bot], axis=-2)


def _pairs_to_state(y):
    nb, npair = y.shape[:2]
    n = HEAD_A
    s = jnp.stack([y[:, :, :n, :n], y[:, :, n:, n:]], axis=2)
    return s.reshape(nb, 2 * npair, n, n)


def _cast_weight_once(w_ref, wb_ref):
    @pl.when(pl.program_id(0) == 0)
    def _():
        wb_ref[...] = w_ref[...].astype(BF16)


def _weight_spec(k, n, col):
    return pl.BlockSpec((k, n), lambda i: (0, col), pipeline_mode=pl.Buffered(1))


def _mm_res_kernel(x_ref, w_ref, r_ref, g_ref, o_ref, wb_ref, *, final_norm):
    _cast_weight_once(w_ref, wb_ref)
    y = r_ref[...] + jnp.dot(x_ref[...].astype(BF16), wb_ref[...], preferred_element_type=F32)
    if final_norm:
        y = _rms(y, g_ref[...])
    o_ref[...] = y


def _mm_res(x, w, res, gain=None):
    m, k = x.shape
    n = w.shape[1]
    bm = min(ROW_TILE, m)
    assert m % bm == 0
    final_norm = gain is not None
    g = (gain if final_norm else jnp.ones((n,), F32)).reshape(1, n)
    return pl.pallas_call(
        functools.partial(_mm_res_kernel, final_norm=final_norm),
        grid=(m // bm,),
        in_specs=[pl.BlockSpec((bm, k), lambda i: (i, 0)),
                  _weight_spec(k, n, 0),
                  pl.BlockSpec((bm, n), lambda i: (i, 0)),
                  pl.BlockSpec((1, n), lambda i: (0, 0))],
        out_specs=pl.BlockSpec((bm, n), lambda i: (i, 0)),
        out_shape=jax.ShapeDtypeStruct((m, n), F32),
        scratch_shapes=[pltpu.VMEM((k, n), BF16)],
        compiler_params=_params("arbitrary"),
        name="mm_res",
    )(x, w, res, g)


def _norm_mm_kernel(x_ref, g_ref, w_ref, o_ref, wb_ref):
    _cast_weight_once(w_ref, wb_ref)
    xn = _rms(x_ref[...], g_ref[...]).astype(BF16)
    o_ref[...] = jnp.dot(xn, wb_ref[...], preferred_element_type=F32)


def _norm_mm(x, gain, w, col, n):
    m, k = x.shape
    bm = min(ROW_TILE, m)
    assert m % bm == 0 and w.shape[1] % n == 0
    return pl.pallas_call(
        _norm_mm_kernel,
        grid=(m // bm,),
        in_specs=[pl.BlockSpec((bm, k), lambda i: (i, 0)),
                  pl.BlockSpec((1, k), lambda i: (0, 0)),
                  _weight_spec(k, n, col)],
        out_specs=pl.BlockSpec((bm, n), lambda i: (i, 0)),
        out_shape=jax.ShapeDtypeStruct((m, n), F32),
        scratch_shapes=[pltpu.VMEM((k, n), BF16)],
        compiler_params=_params("arbitrary"),
        name="norm_mm",
    )(x, gain.reshape(1, k), w)


SB_BLOCK = 256
SB_HEADS = 4
SB_CUM_PIECES = 1
SBS_PAGES = 8


def _sbp_kernel(bias_ref, q_ref, k_ref, v_ref, g_ref, o_ref, kb_ref, vb_ref, *, blk, scale, heads):
    hg = pl.program_id(1)
    qi = pl.program_id(2)

    @pl.when(qi == 0)
    def _():
        kb_ref[...] = k_ref[...].astype(BF16)
        vb_ref[...] = v_ref[...].astype(BF16)

    lanes = [slice(h * HEAD_B, (h + 1) * HEAD_B) for h in range(heads)]
    biases = [bias_ref[hg * heads + h] * LOG2E for h in range(heads)]
    qs = [(q_ref[:, ln] * (scale * LOG2E)).astype(BF16) for ln in lanes]
    ri = lax.broadcasted_iota(jnp.int32, (blk, blk), 0)
    ci = lax.broadcasted_iota(jnp.int32, (blk, blk), 1)
    suffix = jnp.where(ri > ci, 1.0, 0.0).astype(BF16)
    causal = ci < ri

    def block(j, carries, accs, diag):
        rows = pl.ds(pl.multiple_of(j * blk, blk), blk)
        zs = [lax.dot_general(q, kb_ref[rows, ln], NT, preferred_element_type=F32) + b
              for q, ln, b in zip(qs, lanes, biases)]
        sps = [_softplus2(z) for z in zs]
        lks = [jnp.where(causal, -sp, 0.0) if diag else -sp for sp in sps]
        rests = [_dot(lk, suffix, pa=SB_CUM_PIECES) for lk in lks]
        atts = [jnp.exp2(z - sp + rs + c) for z, sp, rs, c in zip(zs, sps, rests, carries)]
        if diag:
            atts = [jnp.where(causal, att, 0.0) for att in atts]
        accs = [acc + jnp.dot(att.astype(BF16), vb_ref[rows, ln], preferred_element_type=F32)
                for acc, att, ln in zip(accs, atts, lanes)]
        carries = [c + jnp.sum(lk, axis=1, keepdims=True) for c, lk in zip(carries, lks)]
        return carries, accs

    carries = [jnp.zeros((blk, 1), F32)] * heads
    accs = [jnp.zeros((blk, HEAD_B), F32)] * heads
    carries, accs = block(qi, carries, accs, True)

    def body(it, ca):
        cs, as_ = block(qi - 1 - it, list(ca[:heads]), list(ca[heads:]), False)
        return tuple(cs) + tuple(as_)

    ca = lax.fori_loop(0, qi, body, tuple(carries) + tuple(accs))
    for h, ln in enumerate(lanes):
        o_ref[:, ln] = (ca[heads + h] * _silu(g_ref[:, ln])).astype(o_ref.dtype)


def _sb_prompt(q, k, v, g, bias, seq):
    m, d = q.shape
    nb = m // seq
    nh = d // HEAD_B
    heads = math.gcd(nh, SB_HEADS)
    width = heads * HEAD_B
    blk = min(SB_BLOCK, seq)
    assert seq % blk == 0
    nq = seq // blk
    qmap = lambda b, h, i: (b * nq + i, h)
    kmap = lambda b, h, i: (b, h)
    kern = functools.partial(_sbp_kernel, blk=blk, scale=1.0 / math.sqrt(HEAD_B), heads=heads)
    return pl.pallas_call(
        kern,
        grid=(nb, nh // heads, nq),
        in_specs=[pl.BlockSpec(memory_space=pltpu.SMEM),
                  pl.BlockSpec((blk, width), qmap),
                  pl.BlockSpec((seq, width), kmap),
                  pl.BlockSpec((seq, width), kmap),
                  pl.BlockSpec((blk, width), qmap)],
        out_specs=pl.BlockSpec((blk, width), qmap),
        out_shape=jax.ShapeDtypeStruct((m, d), BF16),
        scratch_shapes=[pltpu.VMEM((seq, width), BF16), pltpu.VMEM((seq, width), BF16)],
        compiler_params=_params("arbitrary", "arbitrary", "arbitrary"),
        name="sb_prompt",
    )(bias, q, k, v, g)


def _wide_page(refs, page):
    ht = refs[0].shape[1]
    rows = [r.reshape(page * ht, HEAD_B) for r in refs]
    return jnp.concatenate([r[pl.ds(j, page, stride=ht), :].astype(BF16) for r in rows for j in range(ht)],
                           axis=1)


def _sbs_kernel(pt_ref, qbd_ref, bias_ref, kn_ref, vn_ref, g_ref, *rest, ts, page, nh, nkv):
    del pt_ref
    k_refs, v_refs = rest[:nkv], rest[nkv:2 * nkv]
    o_ref, acc_ref, carry_ref = rest[2 * nkv:]
    ntile = nh // k_refs[0].shape[1]
    p = pl.program_id(1)
    last = pl.num_programs(1) - 1
    cols = nh * ts
    qbd = qbd_ref[...]
    bias = bias_ref[...]
    ri = lax.broadcasted_iota(jnp.int32, (page, page), 0)
    ci = lax.broadcasted_iota(jnp.int32, (page, page), 1)
    suffix = jnp.where(ci > ri, 1.0, 0.0).astype(BF16)

    wide = functools.partial(_wide_page, page=page)

    def process(ks, vs, mask):
        zs = [jnp.dot(k, qbd, preferred_element_type=F32) + bias for k in ks]
        sps = [_softplus2(z) for z in zs]
        lks = [-sp if mask is None else jnp.where(mask, -sp, 0.0) for sp in sps]
        rests = [_dot(suffix, lk, pb=SB_CUM_PIECES) for lk in lks]
        carry = carry_ref[...]
        atts = []
        for z, sp, lk, rs in zip(zs, sps, lks, rests):
            att = jnp.exp2(z - sp + rs + carry)
            if mask is not None:
                att = jnp.where(mask, att, 0.0)
            atts.append(att.astype(BF16))
            carry = carry + jnp.sum(lk, axis=0, keepdims=True)
        att = atts[0] if len(atts) == 1 else jnp.concatenate(atts, axis=0)
        val = vs[0] if len(vs) == 1 else jnp.concatenate(vs, axis=0)
        acc_ref[...] += lax.dot_general(att, val, TN, preferred_element_type=F32)
        carry_ref[...] = carry

    @pl.when(p == 0)
    def _():
        acc_ref[...] = jnp.zeros_like(acc_ref)
        carry_ref[...] = jnp.zeros_like(carry_ref)
        pad = jnp.zeros((page - ts, kn_ref.shape[-1]), BF16)
        kn = jnp.concatenate([kn_ref[...].astype(BF16), pad], axis=0)
        vn = jnp.concatenate([vn_ref[...].astype(BF16), pad], axis=0)
        krow = lax.broadcasted_iota(jnp.int32, (page, cols), 0)
        qcol = lax.broadcasted_iota(jnp.int32, (page, cols), 1) & (ts - 1)
        process([kn], [vn], krow < qcol)

    @pl.when(p > 0)
    def _():
        pages = range(0, nkv, ntile)
        process([wide(k_refs[j:j + ntile]) for j in pages], [wide(v_refs[j:j + ntile]) for j in pages], None)

    @pl.when(p == last)
    def _():
        g = g_ref[...]
        for h in range(nh):
            sl = slice(h * HEAD_B, (h + 1) * HEAD_B)
            o = acc_ref[h * ts:(h + 1) * ts, sl]
            o_ref[:, sl] = (o * _silu(g[:, sl])).astype(o_ref.dtype)


def _sb_sample(q, k_new, v_new, g, cache_k, cache_v, page_table, bias, ts):
    m, d = q.shape
    nb = m // ts
    nh = d // HEAD_B
    n_pool, page = cache_k.shape[:2]
    n_pages = page_table.shape[1]
    cols = nh * ts
    assert ts % SUBLANES == 0 and ts & (ts - 1) == 0
    scale2 = LOG2E / math.sqrt(HEAD_B)
    q4 = (q * scale2).reshape(nb, ts, nh, HEAD_B)
    eye = jnp.eye(nh, dtype=F32)
    qbd = jnp.einsum("bihc,hg->bhcgi", q4, eye).reshape(nb, d, cols).astype(BF16)
    bias_cols = jnp.repeat(bias * LOG2E, ts).reshape(1, cols)
    ht = math.gcd(nh, SUBLANES)
    ntile = nh // ht
    npg = math.gcd(n_pages, SBS_PAGES)
    nkv = npg * ntile
    tok = lambda b, p, pt: (b, 0, 0)

    def page_spec(j, t):
        return pl.BlockSpec((None, page, ht, HEAD_B),
                            lambda b, p, pt: (pt[b, n_pages - 1 - (jnp.maximum(p, 1) - 1) * npg - j], 0, t, 0))

    page_specs = [page_spec(j, t) for j in range(npg) for t in range(ntile)]
    grid_spec = pltpu.PrefetchScalarGridSpec(
        num_scalar_prefetch=1,
        grid=(nb, n_pages // npg + 1),
        in_specs=[pl.BlockSpec((None, d, cols), tok),
                  pl.BlockSpec((1, cols), lambda b, p, pt: (0, 0)),
                  pl.BlockSpec((None, ts, d), tok),
                  pl.BlockSpec((None, ts, d), tok),
                  pl.BlockSpec((None, ts, d), tok)] + page_specs + page_specs,
        out_specs=pl.BlockSpec((None, ts, d), tok),
        scratch_shapes=[pltpu.VMEM((cols, d), F32), pltpu.VMEM((1, cols), F32)],
    )
    kern = functools.partial(_sbs_kernel, ts=ts, page=page, nh=nh, nkv=nkv)
    out = pl.pallas_call(
        kern,
        grid_spec=grid_spec,
        out_shape=jax.ShapeDtypeStruct((nb, ts, d), F32),
        compiler_params=_params("arbitrary", "arbitrary"),
        name="sb_sample",
    )(page_table, qbd, bias_cols, k_new.reshape(nb, ts, d), v_new.reshape(nb, ts, d), g.reshape(nb, ts, d),
      *([cache_k] * nkv), *([cache_v] * nkv))
    return out.reshape(m, d)


def _pad_lora(w_in, w_out):
    r = w_in.shape[1]
    w_in = jnp.pad(w_in, ((0, 0), (0, LORA_PAD - r))).astype(BF16)
    w_out = jnp.pad(w_out, ((0, LORA_PAD - r), (0, 0))).astype(BF16)
    return w_in, w_out


def kernel(x_prompt, x_sample, state_shift, state_wkv, cache_k, cache_v, page_table, a_norm, a_mu, a_w_rkvg, a_w0, a_w1, a_w2, a_a0, a_a1, a_a2, a_k_k, a_k_a, a_r_k, a_lnx_g, a_lnx_b, a_w_o, kv_norm, w_kv, b_norm, b_w_qg, b_logit_bias, b_w_o, final_norm):
    nbp, tp, d = x_prompt.shape
    nbs, ts, _ = x_sample.shape
    n_a = a_norm.shape[0]
    n_b = b_norm.shape[0]
    nh_a = d // HEAD_A
    nh_b = d // HEAD_B

    hp = x_prompt.reshape(nbp * tp, d)
    hs = x_sample.reshape(nbs * ts, d)
    shift_p, wkv_p, shift_s, wkv_s = [], [], [], []

    for i in range(n_a):
        w1, w2 = _pad_lora(a_w1[i], a_w2[i])
        a1, a2 = _pad_lora(a_a1[i], a_a2[i])
        w_rkvg = a_w_rkvg[i]
        w_o = a_w_o[i]
        r_k = a_r_k[i].reshape(d)

        def a_layer(h_in, seq, shift0, s0):
            xs, h_last, lw, a = _norm_shift_lora(h_in, shift0, seq, a_norm[i], a_mu[i], w1, w2, a_w0[i],
                                             a1, a2, a_a0[i])
            rkvg = _bmm(xs, w_rkvg)
            og, y = _wkv(rkvg, lw, a, _state_to_pairs(s0.astype(F32)), seq,
                         a_k_k[i], a_k_a[i], r_k, a_lnx_g[i], a_lnx_b[i])
            h_out = _mm_res(og, w_o, h_in)
            return h_out, h_last, _pairs_to_state(y).astype(s0.dtype)

        hp, sh, st = a_layer(hp, tp, jnp.zeros((nbp, d), F32),
                             jnp.zeros((nbp,) + state_wkv.shape[2:], state_wkv.dtype))
        shift_p.append(sh); wkv_p.append(st)
        hs, sh, st = a_layer(hs, ts, state_shift[i], state_wkv[i])
        shift_s.append(sh); wkv_s.append(st)

    k_p = _norm_mm(hp, kv_norm, w_kv, 0, d)
    v_p = _norm_mm(hp, kv_norm, w_kv, 1, d)
    k_s = _norm_mm(hs, kv_norm, w_kv, 0, d)
    v_s = _norm_mm(hs, kv_norm, w_kv, 1, d)

    for j in range(n_b):
        w_qg = b_w_qg[j]
        w_o = b_w_o[j]
        gain = final_norm if j == n_b - 1 else None
        qp = _norm_mm(hp, b_norm[j], w_qg, 0, d)
        gp = _norm_mm(hp, b_norm[j], w_qg, 1, d)
        op = _sb_prompt(qp, k_p, v_p, gp, b_logit_bias[j], tp)
        hp = _mm_res(op, w_o, hp, gain)
        qs = _norm_mm(hs, b_norm[j], w_qg, 0, d)
        gs = _norm_mm(hs, b_norm[j], w_qg, 1, d)
        os_ = _sb_sample(qs, k_s, v_s, gs, cache_k, cache_v, page_table, b_logit_bias[j], ts)
        hs = _mm_res(os_, w_o, hs, gain)

    if n_b == 0:
        raise NotImplementedError("trunk without stick-breaking layers")

    return (hp.reshape(nbp, tp, d), hs.reshape(nbs, ts, d),
            jnp.stack(shift_p), jnp.stack(wkv_p),
            k_p.reshape(nbp, tp, nh_b, HEAD_B), v_p.reshape(nbp, tp, nh_b, HEAD_B),
            jnp.stack(shift_s), jnp.stack(wkv_s),
            k_s.reshape(nbs, ts, nh_b, HEAD_B), v_s.reshape(nbs, ts, nh_b, HEAD_B))
```

```python
import functools
import math

import jax
import jax.numpy as jnp
from jax import lax
from jax.experimental import pallas as pl
from jax.experimental.pallas import tpu as pltpu

F32 = jnp.float32
BF16 = jnp.bfloat16

HEAD_A = 64
HEAD_A_SHIFT = 6
HEAD_B = 128
LANES = 128
SUBLANES = 8
LOG2E = 1.4426950408889634
RMS_EPS = 1e-6
LNX_EPS = 64e-5
LORA_PAD = 128
WKV_CHUNK = 64
WKV_SUBCHUNKS = 4
WKV_PAIRS = 16
WKV_P_MX = 1
WKV_P_INV = 1
WKV_P_APPLY = 1
WKV_P_STATE = 1
VMEM_LIMIT = 56 * 1024 * 1024
ROW_TILE = 512
COL_TILE = 512

NN = (((1,), (0,)), ((), ()))
NT = (((1,), (1,)), ((), ()))
TN = (((0,), (0,)), ((), ()))


def _split(x, n):
    parts = []
    for _ in range(n - 1):
        hi = x.astype(BF16)
        parts.append(hi)
        x = x - hi.astype(F32)
    parts.append(x.astype(BF16))
    return parts


def _dot(a, b, dims=NN, pa=1, pb=1):
    ap = _split(a, pa) if pa > 1 or a.dtype != BF16 else [a]
    bp = _split(b, pb) if pb > 1 or b.dtype != BF16 else [b]
    n = max(pa, pb)
    acc = None
    for j in reversed(range(len(bp))):
        lhs = [ap[i] for i in reversed(range(len(ap))) if i + j < n]
        if dims == TN or len(lhs) == 1:
            terms = [lax.dot_general(x, bp[j], dims, preferred_element_type=F32) for x in lhs]
        else:
            m = a.shape[0]
            t = lax.dot_general(jnp.concatenate(lhs, axis=0), bp[j], dims, preferred_element_type=F32)
            terms = [t[k * m:(k + 1) * m] for k in range(len(lhs))]
        for t in terms:
            acc = t if acc is None else acc + t
    return acc


def _dotp(a, b, dims, level):
    return _dot(a, b, dims, pa=level, pb=level)


def _softplus2(z2):
    return jnp.maximum(z2, 0.0) + jnp.log(1.0 + jnp.exp2(-jnp.abs(z2))) * LOG2E


def _silu(g):
    return g / (1.0 + jnp.exp(-g))


def _rms(x, gain):
    return x * lax.rsqrt(jnp.mean(x * x, axis=-1, keepdims=True) + RMS_EPS) * gain


def _params(*sem):
    return pltpu.CompilerParams(dimension_semantics=sem, vmem_limit_bytes=VMEM_LIMIT)


def _token_shift(x_ref, xh_ref, sh_ref, gain, i, *, bm, seq, per_row_start):
    h = _rms(x_ref[...], gain)
    rolled = pltpu.roll(h, 1, 0)
    row = lax.broadcasted_iota(jnp.int32, (bm, 1), 0)
    if per_row_start:
        h_prev = jnp.where((row & (seq - 1)) == 0, sh_ref[...], rolled)
    else:
        halo = _rms(xh_ref[0], gain)
        first = jnp.where((i * bm) % seq == 0, sh_ref[...], halo[SUBLANES - 1:SUBLANES])
        h_prev = jnp.where(row == 0, first, rolled)
    return h, h_prev - h


def _shift_operands(x, shift, seq, bm, per_row_start, tile_index):
    m, d = x.shape
    nb = m // seq
    if per_row_start:
        sh = jnp.repeat(shift, seq, axis=0)
        sh_spec = pl.BlockSpec((bm, d), lambda *g: (tile_index(*g), 0))
    else:
        sh = shift.reshape(nb, 1, d)
        sh_spec = pl.BlockSpec((None, 1, d), lambda *g: ((tile_index(*g) * bm) // seq, 0, 0))
    xh = x.reshape(m // SUBLANES, SUBLANES, d)
    specs = [pl.BlockSpec((bm, d), lambda *g: (tile_index(*g), 0)),
             pl.BlockSpec((1, SUBLANES, d),
                          lambda *g: (jnp.maximum(tile_index(*g) * (bm // SUBLANES) - 1, 0), 0, 0)),
             sh_spec]
    return [x, xh, sh], specs


def _nsl_kernel(x_ref, xh_ref, sh_ref, gain_ref, mu_ref, w1_ref, w2_ref, w0_ref,
                a1_ref, a2_ref, a0_ref, h_ref, lw_ref, a_ref, *, bm, seq, per_row_start):
    h, dx = _token_shift(x_ref, xh_ref, sh_ref, gain_ref[...], pl.program_id(0),
                         bm=bm, seq=seq, per_row_start=per_row_start)
    h_ref[...] = h if per_row_start else h[bm - SUBLANES:]
    x4 = (h + dx * mu_ref[4:5, :]).astype(BF16)
    t = jnp.tanh(jnp.dot(x4, w1_ref[...], preferred_element_type=F32)).astype(BF16)
    wl = jnp.dot(t, w2_ref[...], preferred_element_type=F32) + w0_ref[...]
    lw_ref[...] = -math.exp(-0.5) / (1.0 + jnp.exp(-wl))
    x5 = (h + dx * mu_ref[5:6, :]).astype(BF16)
    u = jnp.dot(x5, a1_ref[...], preferred_element_type=F32).astype(BF16)
    al = jnp.dot(u, a2_ref[...], preferred_element_type=F32) + a0_ref[...]
    a_ref[...] = 1.0 / (1.0 + jnp.exp(-al))


def _shift_tiling(m, seq, row_tile):
    per_row_start = seq < SUBLANES * 2
    bm = m if per_row_start else min(row_tile, seq)
    assert m % bm == 0 and (per_row_start or seq % bm == 0)
    return per_row_start, bm


def _norm_shift_lora(x, shift, seq, gain, mu, w1, w2, w0, a1, a2, a0):
    m, d = x.shape
    per_row_start, bm = _shift_tiling(m, seq, ROW_TILE)
    nb = m // seq
    operands, specs = _shift_operands(x, shift, seq, bm, per_row_start, lambda i: i)
    row = lambda i: (i, 0)
    const = lambda i: (0, 0)
    kern = functools.partial(_nsl_kernel, bm=bm, seq=seq, per_row_start=per_row_start)
    if per_row_start:
        h_spec, h_shape = pl.BlockSpec((bm, d), row), (m, d)
    else:
        h_spec, h_shape = pl.BlockSpec((None, SUBLANES, d), lambda i: (i, 0, 0)), (m // bm, SUBLANES, d)
    h_tail, lw, a = pl.pallas_call(
        kern,
        grid=(m // bm,),
        in_specs=specs + [
            pl.BlockSpec((1, d), const),
            pl.BlockSpec(mu.shape, const),
            pl.BlockSpec(w1.shape, const),
            pl.BlockSpec(w2.shape, const),
            pl.BlockSpec((1, d), const),
            pl.BlockSpec(a1.shape, const),
            pl.BlockSpec(a2.shape, const),
            pl.BlockSpec((1, d), const),
        ],
        out_specs=[
            h_spec,
            pl.BlockSpec((bm, d), row),
            pl.BlockSpec((bm, d), row),
        ],
        out_shape=[
            jax.ShapeDtypeStruct(h_shape, F32),
            jax.ShapeDtypeStruct((m, d), F32),
            jax.ShapeDtypeStruct((m, d), F32),
        ],
        compiler_params=_params("arbitrary"),
        name="norm_shift_lora",
    )(*operands, gain.reshape(1, d), mu, w1, w2, w0.reshape(1, d), a1, a2, a0.reshape(1, d))
    if per_row_start:
        last = h_tail.reshape(nb, seq, d)[:, -1]
    else:
        last = h_tail.reshape(nb, seq // bm, SUBLANES, d)[:, -1, -1]
    return last, lw, a


def _proj_kernel(x_ref, xh_ref, sh_ref, gain_ref, mu_ref, w_ref, o_ref, *wb_ref, bm, seq, per_row_start):
    p = pl.program_id(0)
    t = pl.program_id(1)
    if wb_ref:
        @pl.when(t == 0)
        def _():
            wb_ref[0][...] = w_ref[...].astype(BF16)
        w = wb_ref[0][...]
        tile = t
    else:
        w = w_ref[...].astype(BF16)
        tile = 0
    h, dx = _token_shift(x_ref, xh_ref, sh_ref, gain_ref[...], tile, bm=bm, seq=seq, per_row_start=per_row_start)
    xs = (h + dx * mu_ref[pl.ds(p, 1), :]).astype(BF16)
    o_ref[...] = jnp.dot(xs, w, preferred_element_type=F32)


def _proj(x, shift, seq, gain, mu, w):
    m, d = x.shape
    np_, k, n = w.shape
    per_row_start, bm = _shift_tiling(m, seq, ROW_TILE)
    kern = functools.partial(_proj_kernel, bm=bm, seq=seq, per_row_start=per_row_start)
    const = lambda p, t: (0, 0)
    if m == bm and n % COL_TILE == 0:
        operands, specs = _shift_operands(x, shift, seq, bm, per_row_start, lambda p, j: 0)
        grid = (np_, n // COL_TILE)
        w_spec = pl.BlockSpec((None, k, COL_TILE), lambda p, j: (p, 0, j))
        o_spec = pl.BlockSpec((None, m, COL_TILE), lambda p, j: (p, 0, j))
        scratch = []
    else:
        operands, specs = _shift_operands(x, shift, seq, bm, per_row_start, lambda p, i: i)
        grid = (np_, m // bm)
        w_spec = pl.BlockSpec((None, k, n), lambda p, i: (p, 0, 0), pipeline_mode=pl.Buffered(1))
        o_spec = pl.BlockSpec((None, bm, n), lambda p, i: (p, i, 0))
        scratch = [pltpu.VMEM((k, n), BF16)]
    return pl.pallas_call(
        kern,
        grid=grid,
        in_specs=specs + [pl.BlockSpec((1, d), const), pl.BlockSpec(mu.shape, const), w_spec],
        out_specs=o_spec,
        out_shape=jax.ShapeDtypeStruct((np_, m, n), F32),
        scratch_shapes=scratch,
        compiler_params=_params("arbitrary", "arbitrary"),
        name="proj",
    )(*operands, gain.reshape(1, d), mu, w)


def _wkv_kernel(r_ref, k_ref, v_ref, g_ref, w_ref, a_ref, kk_ref, ka_ref, rk_ref, lg_ref, lb_ref,
                s0_ref, o_ref, so_ref, y_ref, *, chunk, rows, pairs, sub):
    c = pl.program_id(2)
    nc = pl.num_programs(2)
    C = chunk
    C2 = 2 * C

    @pl.when(c == 0)
    def _():
        zero = jnp.zeros((HEAD_A, HEAD_A), F32)
        for p in range(pairs):
            top = jnp.concatenate([s0_ref[2 * p], zero], axis=1)
            bot = jnp.concatenate([zero, s0_ref[2 * p + 1]], axis=1)
            y_ref[p] = jnp.concatenate([top, bot], axis=0)

    lane = lax.broadcasted_iota(jnp.int32, (1, LANES), 1)
    head0 = lane < HEAD_A
    ri = lax.broadcasted_iota(jnp.int32, (C2, C2), 0)
    ci = lax.broadcasted_iota(jnp.int32, (C2, C2), 1)
    same_head = (ri >> HEAD_A_SHIFT) == (ci >> HEAD_A_SHIFT)
    si = ri & (C - 1)
    sj = ci & (C - 1)
    strict = sj < si
    incl = sj <= si
    eye = jnp.where(ri == ci, 1.0, 0.0)
    ti = lax.broadcasted_iota(jnp.int32, (C, C), 0)
    tj = lax.broadcasted_iota(jnp.int32, (C, C), 1)
    tril = jnp.where(tj <= ti, 1.0, 0.0).astype(BF16)

    def stack(x):
        return jnp.concatenate([jnp.where(head0, x, 0.0), jnp.where(head0, 0.0, x)], axis=0)

    def pack(x):
        return x[:C] + x[C:]

    def head_sum(x):
        s0 = jnp.sum(jnp.where(head0, x, 0.0), axis=1, keepdims=True)
        s1 = jnp.sum(jnp.where(head0, 0.0, x), axis=1, keepdims=True)
        return jnp.where(head0, s0, s1)

    def pair_chain(p, tok):
        ln = slice(p * LANES, (p + 1) * LANES)

        def load(ref):
            x = ref[tok, ln]
            if rows < C:
                x = jnp.concatenate([x, jnp.zeros((C - rows, LANES), F32)], axis=0)
            return x

        R, K, V, Wl, A = (load(ref) for ref in (r_ref, k_ref, v_ref, w_ref, a_ref))

        kkp = K * kk_ref[:, ln]
        kk = kkp / jnp.maximum(jnp.sqrt(head_sum(kkp * kkp)), 1e-12)
        Km = K * (1.0 + (A - 1.0) * ka_ref[:, ln])

        Lc = _dot(tril, Wl, pb=3)
        yield
        Lend = Lc[C - 1:C, :]
        g_inv = jnp.exp(-Lc)
        g_rem = jnp.exp(Lend - Lc)
        kka = kk * A
        Ah = -kk * jnp.exp(Lc - Wl)
        Rh = R * jnp.exp(Lc)
        Bc = kka * g_inv
        Kc = Km * g_inv
        Bt = kka * g_rem
        Kt = Km * g_rem

        sAh = stack(Ah)
        lhs = jnp.concatenate([sAh, stack(Rh)], axis=0)
        rhs = jnp.concatenate([stack(Bc), stack(Kc)], axis=0)
        mx = _dotp(lhs, rhs, NT, WKV_P_MX)
        yield
        Nab =jnp.where(strict, mx[:C2, :C2], 0.0)
        nrest = jnp.concatenate([jnp.where(strict, mx[:C2, C2:], 0.0),
                                 jnp.where(incl, mx[C2:, C2:], 0.0)], axis=0)
        Nrb = jnp.where(incl, mx[C2:, :C2], 0.0)

        Tm = eye + Nab
        Pw = Nab
        nv = _dotp(nrest, stack(V), NN, WKV_P_APPLY)
        yield
        nsq = int(math.log2(C)) - 1
        Pw = _dotp(Pw, Pw, NN, WKV_P_INV)
        yield
        for it in range(nsq):
            if it < nsq - 1:
                both = _dotp(jnp.concatenate([Pw, Tm], axis=0), Pw, NN, WKV_P_INV)
                Pw, Tm = both[:C2], Tm + both[C2:]
            else:
                Tm = Tm + _dotp(Tm, Pw, NN, WKV_P_INV)
            yield

        tx = _dotp(Tm, jnp.concatenate([sAh, nv[:C2]], axis=1), NN, WKV_P_APPLY)
        yield

        Y = y_ref[p]
        sy = _dotp(jnp.concatenate([tx[:, :LANES], Rh], axis=0), Y, NT, WKV_P_STATE)
        yield
        Ub = sy[:C2] + tx[:, LANES:]
        U = pack(Ub)
        O = sy[C2:] + pack(_dotp(Nrb, Ub, NN, WKV_P_STATE) + nv[C2:])
        upd = _dotp(jnp.concatenate([U, V], axis=0), jnp.concatenate([Bt, Kt], axis=0), TN, WKV_P_STATE)
        yield
        y_new = Y * jnp.exp(Lend) + jnp.where(same_head, upd, 0.0)

        mean = head_sum(O) * (1.0 / HEAD_A)
        dlt = O - mean
        var = head_sum(dlt * dlt) * (1.0 / HEAD_A)
        on = dlt * lax.rsqrt(var + LNX_EPS) * lg_ref[:, ln] + lb_ref[:, ln]
        bonus = head_sum(R * Km * rk_ref[:, ln]) * V
        res = (on + bonus) * _silu(load(g_ref))
        yield
        y_ref[p] = y_new
        o_ref[tok, ln] = res[:rows].astype(o_ref.dtype)

    def chunk_step(s, carry):
        tok = pl.ds(pl.multiple_of(s * rows, rows), rows)
        chains = [pair_chain(p, tok) for p in range(pairs)]
        while chains:
            chains = [ch for ch in chains if next(ch, True) is None]
        return carry

    lax.fori_loop(0, sub, chunk_step, 0)

    @pl.when(c == nc - 1)
    def _():
        for p in range(pairs):
            y = y_ref[p]
            so_ref[2 * p] = y[:HEAD_A, :HEAD_A]
            so_ref[2 * p + 1] = y[HEAD_A:, HEAD_A:]


def _wkv(rkvg, lw, a, s0, seq, k_k, k_a, r_k, lnx_g, lnx_b):
    _, m, d = rkvg.shape
    nb = m // seq
    npair = d // LANES
    pairs = math.gcd(npair, WKV_PAIRS)
    width = pairs * LANES
    rows = min(WKV_CHUNK, seq)
    assert seq % rows == 0 and rows % SUBLANES == 0
    sub = math.gcd(seq // rows, WKV_SUBCHUNKS)
    blk = sub * rows
    nc = seq // blk
    tok = lambda b, h, c: (b * nc + c, h)
    par = lambda b, h, c: (0, h)
    st = lambda b, h, c: (b, h, 0, 0)

    def proj(p):
        return pl.BlockSpec((None, blk, width), lambda b, h, c: (p, b * nc + c, h))

    vec = lambda x: x.reshape(1, d)
    kern = functools.partial(_wkv_kernel, chunk=WKV_CHUNK, rows=rows, pairs=pairs, sub=sub)
    return pl.pallas_call(
        kern,
        grid=(nb, npair // pairs, nc),
        in_specs=[proj(0), proj(1), proj(2), proj(3),
                  pl.BlockSpec((blk, width), tok), pl.BlockSpec((blk, width), tok),
                  pl.BlockSpec((1, width), par), pl.BlockSpec((1, width), par),
                  pl.BlockSpec((1, width), par), pl.BlockSpec((1, width), par),
                  pl.BlockSpec((1, width), par),
                  pl.BlockSpec((None, 2 * pairs, HEAD_A, HEAD_A), st)],
        out_specs=[pl.BlockSpec((blk, width), tok),
                   pl.BlockSpec((None, 2 * pairs, HEAD_A, HEAD_A), st)],
        out_shape=[jax.ShapeDtypeStruct((m, d), BF16 if rows % (2 * SUBLANES) == 0 else F32),
                   jax.ShapeDtypeStruct((nb, 2 * npair, HEAD_A, HEAD_A), F32)],
        scratch_shapes=[pltpu.VMEM((pairs, LANES, LANES), F32)],
        compiler_params=_params("arbitrary", "arbitrary", "arbitrary"),
        name="wkv",
    )(rkvg, rkvg, rkvg, rkvg, lw, a, vec(k_k), vec(k_a), vec(r_k), vec(lnx_g), vec(lnx_b), s0)


def _cast_weight_once(w_ref, wb_ref):
    @pl.when(pl.program_id(0) == 0)
    def _():
        wb_ref[...] = w_ref[...].astype(BF16)


def _weight_spec(k, n, col):
    return pl.BlockSpec((k, n), lambda i: (0, col), pipeline_mode=pl.Buffered(1))


def _mm_res_kernel(x_ref, w_ref, r_ref, g_ref, o_ref, wb_ref, *, final_norm):
    _cast_weight_once(w_ref, wb_ref)
    y = r_ref[...] + jnp.dot(x_ref[...].astype(BF16), wb_ref[...], preferred_element_type=F32)
    if final_norm:
        y = _rms(y, g_ref[...])
    o_ref[...] = y


def _mm_res_cols_kernel(x_ref, w_ref, r_ref, g_ref, o_ref, *, final_norm):
    j = pl.program_id(0)
    cols = pl.ds(pl.multiple_of(j * COL_TILE, COL_TILE), COL_TILE)
    o_ref[:, cols] = r_ref[:, cols] + jnp.dot(x_ref[...].astype(BF16), w_ref[...].astype(BF16),
                                              preferred_element_type=F32)
    if final_norm:
        @pl.when(j == pl.num_programs(0) - 1)
        def _():
            o_ref[...] = _rms(o_ref[...], g_ref[...])


def _mm_res(x, w, res, gain=None):
    m, k = x.shape
    n = w.shape[1]
    bm = min(ROW_TILE, m)
    assert m % bm == 0
    final_norm = gain is not None
    g = (gain if final_norm else jnp.ones((n,), F32)).reshape(1, n)
    if m == bm and n % COL_TILE == 0:
        whole = lambda j: (0, 0)
        return pl.pallas_call(
            functools.partial(_mm_res_cols_kernel, final_norm=final_norm),
            grid=(n // COL_TILE,),
            in_specs=[pl.BlockSpec((m, k), whole),
                      pl.BlockSpec((k, COL_TILE), lambda j: (0, j)),
                      pl.BlockSpec((m, n), whole),
                      pl.BlockSpec((1, n), whole)],
            out_specs=pl.BlockSpec((m, n), whole),
            out_shape=jax.ShapeDtypeStruct((m, n), F32),
            compiler_params=_params("arbitrary"),
            name="mm_res_cols",
        )(x, w, res, g)
    return pl.pallas_call(
        functools.partial(_mm_res_kernel, final_norm=final_norm),
        grid=(m // bm,),
        in_specs=[pl.BlockSpec((bm, k), lambda i: (i, 0)),
                  _weight_spec(k, n, 0),
                  pl.BlockSpec((bm, n), lambda i: (i, 0)),
                  pl.BlockSpec((1, n), lambda i: (0, 0))],
        out_specs=pl.BlockSpec((bm, n), lambda i: (i, 0)),
        out_shape=jax.ShapeDtypeStruct((m, n), F32),
        scratch_shapes=[pltpu.VMEM((k, n), BF16)],
        compiler_params=_params("arbitrary"),
        name="mm_res",
    )(x, w, res, g)


def _norm_mm_kernel(x_ref, g_ref, w_ref, o_ref, wb_ref):
    _cast_weight_once(w_ref, wb_ref)
    xn = _rms(x_ref[...], g_ref[...]).astype(BF16)
    o_ref[...] = jnp.dot(xn, wb_ref[...], preferred_element_type=F32)


def _norm_mm_cols_kernel(x_ref, g_ref, w_ref, o_ref):
    xn = _rms(x_ref[...], g_ref[...]).astype(BF16)
    o_ref[...] = jnp.dot(xn, w_ref[...].astype(BF16), preferred_element_type=F32)


def _norm_mm(x, gain, w, col, n):
    m, k = x.shape
    bm = min(ROW_TILE, m)
    assert m % bm == 0 and w.shape[1] % n == 0
    if m == bm and n % COL_TILE == 0:
        nj = n // COL_TILE
        return pl.pallas_call(
            _norm_mm_cols_kernel,
            grid=(nj,),
            in_specs=[pl.BlockSpec((m, k), lambda j: (0, 0)),
                      pl.BlockSpec((1, k), lambda j: (0, 0)),
                      pl.BlockSpec((k, COL_TILE), lambda j: (0, col * nj + j))],
            out_specs=pl.BlockSpec((m, COL_TILE), lambda j: (0, j)),
            out_shape=jax.ShapeDtypeStruct((m, n), F32),
            compiler_params=_params("arbitrary"),
            name="norm_mm_cols",
        )(x, gain.reshape(1, k), w)
    return pl.pallas_call(
        _norm_mm_kernel,
        grid=(m // bm,),
        in_specs=[pl.BlockSpec((bm, k), lambda i: (i, 0)),
                  pl.BlockSpec((1, k), lambda i: (0, 0)),
                  _weight_spec(k, n, col)],
        out_specs=pl.BlockSpec((bm, n), lambda i: (i, 0)),
        out_shape=jax.ShapeDtypeStruct((m, n), F32),
        scratch_shapes=[pltpu.VMEM((k, n), BF16)],
        compiler_params=_params("arbitrary"),
        name="norm_mm",
    )(x, gain.reshape(1, k), w)


SB_BLOCK = 256
SB_HEADS = 4
SB_CUM_PIECES = 1
SBS_PAGES = 8


def _sbp_kernel(bias_ref, q_ref, k_ref, v_ref, g_ref, o_ref, kb_ref, vb_ref, *, blk, scale, heads):
    hg = pl.program_id(1)
    qi = pl.program_id(2)

    @pl.when(qi == 0)
    def _():
        kb_ref[...] = k_ref[...].astype(BF16)
        vb_ref[...] = v_ref[...].astype(BF16)

    lanes = [slice(h * HEAD_B, (h + 1) * HEAD_B) for h in range(heads)]
    biases = [bias_ref[hg * heads + h] * LOG2E for h in range(heads)]
    qs = [(q_ref[:, ln] * (scale * LOG2E)).astype(BF16) for ln in lanes]
    ri = lax.broadcasted_iota(jnp.int32, (blk, blk), 0)
    ci = lax.broadcasted_iota(jnp.int32, (blk, blk), 1)
    suffix = jnp.where(ri > ci, 1.0, 0.0).astype(BF16)
    causal = ci < ri

    def block(j, carries, accs, diag):
        rows = pl.ds(pl.multiple_of(j * blk, blk), blk)
        zs = [lax.dot_general(q, kb_ref[rows, ln], NT, preferred_element_type=F32) + b
              for q, ln, b in zip(qs, lanes, biases)]
        sps = [_softplus2(z) for z in zs]
        lks = [jnp.where(causal, -sp, 0.0) if diag else -sp for sp in sps]
        rests = [_dot(lk, suffix, pa=SB_CUM_PIECES) for lk in lks]
        atts = [jnp.exp2(z - sp + rs + c) for z, sp, rs, c in zip(zs, sps, rests, carries)]
        if diag:
            atts = [jnp.where(causal, att, 0.0) for att in atts]
        accs = [acc + jnp.dot(att.astype(BF16), vb_ref[rows, ln], preferred_element_type=F32)
                for acc, att, ln in zip(accs, atts, lanes)]
        carries = [c + jnp.sum(lk, axis=1, keepdims=True) for c, lk in zip(carries, lks)]
        return carries, accs

    carries = [jnp.zeros((blk, 1), F32)] * heads
    accs = [jnp.zeros((blk, HEAD_B), F32)] * heads
    carries, accs = block(qi, carries, accs, True)

    def body(it, ca):
        cs, as_ = block(qi - 1 - it, list(ca[:heads]), list(ca[heads:]), False)
        return tuple(cs) + tuple(as_)

    ca = lax.fori_loop(0, qi, body, tuple(carries) + tuple(accs))
    for h, ln in enumerate(lanes):
        o_ref[:, ln] = (ca[heads + h] * _silu(g_ref[:, ln])).astype(o_ref.dtype)


def _sb_prompt(q, k, v, g, bias, seq):
    m, d = q.shape
    nb = m // seq
    nh = d // HEAD_B
    heads = math.gcd(nh, SB_HEADS)
    width = heads * HEAD_B
    blk = min(SB_BLOCK, seq)
    assert seq % blk == 0
    nq = seq // blk
    qmap = lambda b, h, i: (b * nq + i, h)
    kmap = lambda b, h, i: (b, h)
    kern = functools.partial(_sbp_kernel, blk=blk, scale=1.0 / math.sqrt(HEAD_B), heads=heads)
    return pl.pallas_call(
        kern,
        grid=(nb, nh // heads, nq),
        in_specs=[pl.BlockSpec(memory_space=pltpu.SMEM),
                  pl.BlockSpec((blk, width), qmap),
                  pl.BlockSpec((seq, width), kmap),
                  pl.BlockSpec((seq, width), kmap),
                  pl.BlockSpec((blk, width), qmap)],
        out_specs=pl.BlockSpec((blk, width), qmap),
        out_shape=jax.ShapeDtypeStruct((m, d), BF16),
        scratch_shapes=[pltpu.VMEM((seq, width), BF16), pltpu.VMEM((seq, width), BF16)],
        compiler_params=_params("arbitrary", "arbitrary", "arbitrary"),
        name="sb_prompt",
    )(bias, q, k, v, g)


def _wide_page(refs, page):
    ht = refs[0].shape[1]
    rows = [r.reshape(page * ht, HEAD_B) for r in refs]
    return jnp.concatenate([r[pl.ds(j, page, stride=ht), :].astype(BF16) for r in rows for j in range(ht)],
                           axis=1)


def _sbs_kernel(pt_ref, qbd_ref, bias_ref, kn_ref, vn_ref, g_ref, *rest, ts, page, nh, nkv):
    del pt_ref
    k_refs, v_refs = rest[:nkv], rest[nkv:2 * nkv]
    o_ref, acc_ref, carry_ref = rest[2 * nkv:]
    ntile = nh // k_refs[0].shape[1]
    p = pl.program_id(1)
    last = pl.num_programs(1) - 1
    cols = nh * ts
    qbd = qbd_ref[...]
    bias = bias_ref[...]
    ri = lax.broadcasted_iota(jnp.int32, (page, page), 0)
    ci = lax.broadcasted_iota(jnp.int32, (page, page), 1)
    suffix = jnp.where(ci > ri, 1.0, 0.0).astype(BF16)

    wide = functools.partial(_wide_page, page=page)

    def process(ks, vs, mask):
        zs = [jnp.dot(k, qbd, preferred_element_type=F32) + bias for k in ks]
        sps = [_softplus2(z) for z in zs]
        lks = [-sp if mask is None else jnp.where(mask, -sp, 0.0) for sp in sps]
        rests = [_dot(suffix, lk, pb=SB_CUM_PIECES) for lk in lks]
        carry = carry_ref[...]
        atts = []
        for z, sp, lk, rs in zip(zs, sps, lks, rests):
            att = jnp.exp2(z - sp + rs + carry)
            if mask is not None:
                att = jnp.where(mask, att, 0.0)
            atts.append(att.astype(BF16))
            carry = carry + jnp.sum(lk, axis=0, keepdims=True)
        att = atts[0] if len(atts) == 1 else jnp.concatenate(atts, axis=0)
        val = vs[0] if len(vs) == 1 else jnp.concatenate(vs, axis=0)
        acc_ref[...] += lax.dot_general(att, val, TN, preferred_element_type=F32)
        carry_ref[...] = carry

    @pl.when(p == 0)
    def _():
        acc_ref[...] = jnp.zeros_like(acc_ref)
        carry_ref[...] = jnp.zeros_like(carry_ref)
        pad = jnp.zeros((page - ts, kn_ref.shape[-1]), BF16)
        kn = jnp.concatenate([kn_ref[...].astype(BF16), pad], axis=0)
        vn = jnp.concatenate([vn_ref[...].astype(BF16), pad], axis=0)
        krow = lax.broadcasted_iota(jnp.int32, (page, cols), 0)
        qcol = lax.broadcasted_iota(jnp.int32, (page, cols), 1) & (ts - 1)
        process([kn], [vn], krow < qcol)

    @pl.when(p > 0)
    def _():
        pages = range(0, nkv, ntile)
        process([wide(k_refs[j:j + ntile]) for j in pages], [wide(v_refs[j:j + ntile]) for j in pages], None)

    @pl.when(p == last)
    def _():
        g = g_ref[...]
        for h in range(nh):
            sl = slice(h * HEAD_B, (h + 1) * HEAD_B)
            o = acc_ref[h * ts:(h + 1) * ts, sl]
            o_ref[:, sl] = (o * _silu(g[:, sl])).astype(o_ref.dtype)


def _sb_sample(q, k_new, v_new, g, cache_k, cache_v, page_table, bias, ts):
    m, d = q.shape
    nb = m // ts
    nh = d // HEAD_B
    n_pool, page = cache_k.shape[:2]
    n_pages = page_table.shape[1]
    cols = nh * ts
    assert ts % SUBLANES == 0 and ts & (ts - 1) == 0
    scale2 = LOG2E / math.sqrt(HEAD_B)
    q4 = (q * scale2).reshape(nb, ts, nh, HEAD_B)
    eye = jnp.eye(nh, dtype=F32)
    qbd = jnp.einsum("bihc,hg->bhcgi", q4, eye).reshape(nb, d, cols).astype(BF16)
    bias_cols = jnp.repeat(bias * LOG2E, ts).reshape(1, cols)
    ht = math.gcd(nh, SUBLANES)
    ntile = nh // ht
    npg = math.gcd(n_pages, SBS_PAGES)
    nkv = npg * ntile
    tok = lambda b, p, pt: (b, 0, 0)

    def page_spec(j, t):
        return pl.BlockSpec((None, page, ht, HEAD_B),
                            lambda b, p, pt: (pt[b, n_pages - 1 - (jnp.maximum(p, 1) - 1) * npg - j], 0, t, 0))

    page_specs = [page_spec(j, t) for j in range(npg) for t in range(ntile)]
    grid_spec = pltpu.PrefetchScalarGridSpec(
        num_scalar_prefetch=1,
        grid=(nb, n_pages // npg + 1),
        in_specs=[pl.BlockSpec((None, d, cols), tok),
                  pl.BlockSpec((1, cols), lambda b, p, pt: (0, 0)),
                  pl.BlockSpec((None, ts, d), tok),
                  pl.BlockSpec((None, ts, d), tok),
                  pl.BlockSpec((None, ts, d), tok)] + page_specs + page_specs,
        out_specs=pl.BlockSpec((None, ts, d), tok),
        scratch_shapes=[pltpu.VMEM((cols, d), F32), pltpu.VMEM((1, cols), F32)],
    )
    kern = functools.partial(_sbs_kernel, ts=ts, page=page, nh=nh, nkv=nkv)
    out = pl.pallas_call(
        kern,
        grid_spec=grid_spec,
        out_shape=jax.ShapeDtypeStruct((nb, ts, d), F32),
        compiler_params=_params("arbitrary", "arbitrary"),
        name="sb_sample",
    )(page_table, qbd, bias_cols, k_new.reshape(nb, ts, d), v_new.reshape(nb, ts, d), g.reshape(nb, ts, d),
      *([cache_k] * nkv), *([cache_v] * nkv))
    return out.reshape(m, d)


def _pad_lora(w_in, w_out):
    r = w_in.shape[1]
    w_in = jnp.pad(w_in, ((0, 0), (0, LORA_PAD - r))).astype(BF16)
    w_out = jnp.pad(w_out, ((0, LORA_PAD - r), (0, 0))).astype(BF16)
    return w_in, w_out


def kernel(x_prompt, x_sample, state_shift, state_wkv, cache_k, cache_v, page_table, a_norm, a_mu, a_w_rkvg, a_w0, a_w1, a_w2, a_a0, a_a1, a_a2, a_k_k, a_k_a, a_r_k, a_lnx_g, a_lnx_b, a_w_o, kv_norm, w_kv, b_norm, b_w_qg, b_logit_bias, b_w_o, final_norm):
    nbp, tp, d = x_prompt.shape
    nbs, ts, _ = x_sample.shape
    n_a = a_norm.shape[0]
    n_b = b_norm.shape[0]
    nh_a = d // HEAD_A
    nh_b = d // HEAD_B

    hp = x_prompt.reshape(nbp * tp, d)
    hs = x_sample.reshape(nbs * ts, d)
    shift_p, wkv_p, shift_s, wkv_s = [], [], [], []

    for i in range(n_a):
        w1, w2 = _pad_lora(a_w1[i], a_w2[i])
        a1, a2 = _pad_lora(a_a1[i], a_a2[i])
        w_rkvg = a_w_rkvg[i]
        w_o = a_w_o[i]
        r_k = a_r_k[i].reshape(d)

        def a_layer(h_in, seq, shift0, s0):
            h_last, lw, a = _norm_shift_lora(h_in, shift0, seq, a_norm[i], a_mu[i], w1, w2, a_w0[i],
                                             a1, a2, a_a0[i])
            rkvg = _proj(h_in, shift0, seq, a_norm[i], a_mu[i], w_rkvg)
            og, s_out = _wkv(rkvg, lw, a, s0.astype(F32), seq,
                         a_k_k[i], a_k_a[i], r_k, a_lnx_g[i], a_lnx_b[i])
            h_out = _mm_res(og, w_o, h_in)
            return h_out, h_last, s_out.astype(s0.dtype)

        hp, sh, st = a_layer(hp, tp, jnp.zeros((nbp, d), F32),
                             jnp.zeros((nbp,) + state_wkv.shape[2:], state_wkv.dtype))
        shift_p.append(sh); wkv_p.append(st)
        hs, sh, st = a_layer(hs, ts, state_shift[i], state_wkv[i])
        shift_s.append(sh); wkv_s.append(st)

    k_p = _norm_mm(hp, kv_norm, w_kv, 0, d)
    v_p = _norm_mm(hp, kv_norm, w_kv, 1, d)
    k_s = _norm_mm(hs, kv_norm, w_kv, 0, d)
    v_s = _norm_mm(hs, kv_norm, w_kv, 1, d)

    for j in range(n_b):
        w_qg = b_w_qg[j]
        w_o = b_w_o[j]
        gain = final_norm if j == n_b - 1 else None
        qp = _norm_mm(hp, b_norm[j], w_qg, 0, d)
        gp = _norm_mm(hp, b_norm[j], w_qg, 1, d)
        op = _sb_prompt(qp, k_p, v_p, gp, b_logit_bias[j], tp)
        hp = _mm_res(op, w_o, hp, gain)
        qs = _norm_mm(hs, b_norm[j], w_qg, 0, d)
        gs = _norm_mm(hs, b_norm[j], w_qg, 1, d)
        os_ = _sb_sample(qs, k_s, v_s, gs, cache_k, cache_v, page_table, b_logit_bias[j], ts)
        hs = _mm_res(os_, w_o, hs, gain)

    if n_b == 0:
        raise NotImplementedError("trunk without stick-breaking layers")

    return (hp.reshape(nbp, tp, d), hs.reshape(nbs, ts, d),
            jnp.stack(shift_p), jnp.stack(wkv_p),
            k_p.reshape(nbp, tp, nh_b, HEAD_B), v_p.reshape(nbp, tp, nh_b, HEAD_B),
            jnp.stack(shift_s), jnp.stack(wkv_s),
            k_s.reshape(nbs, ts, nh_b, HEAD_B), v_s.reshape(nbs, ts, nh_b, HEAD_B))
```

```python
import functools
import math

import jax
import jax.numpy as jnp
from jax import lax
from jax.experimental import pallas as pl
from jax.experimental.pallas import tpu as pltpu

F32 = jnp.float32
BF16 = jnp.bfloat16

HEAD_A = 64
HEAD_A_SHIFT = 6
HEAD_B = 128
LANES = 128
SUBLANES = 8
LOG2E = 1.4426950408889634
RMS_EPS = 1e-6
LNX_EPS = 64e-5
LORA_PAD = 128
WKV_CHUNK = 64
WKV_SUBCHUNKS = 4
WKV_PAIRS = 16
WKV_P_MX = 1
WKV_P_INV = 1
WKV_P_APPLY = 1
WKV_P_STATE = 1
VMEM_LIMIT = 56 * 1024 * 1024
ROW_TILE = 512
COL_TILE = 512

NN = (((1,), (0,)), ((), ()))
NT = (((1,), (1,)), ((), ()))
TN = (((0,), (0,)), ((), ()))


def _split(x, n):
    parts = []
    for _ in range(n - 1):
        hi = x.astype(BF16)
        parts.append(hi)
        x = x - hi.astype(F32)
    parts.append(x.astype(BF16))
    return parts


def _dot(a, b, dims=NN, pa=1, pb=1):
    ap = _split(a, pa) if pa > 1 or a.dtype != BF16 else [a]
    bp = _split(b, pb) if pb > 1 or b.dtype != BF16 else [b]
    n = max(pa, pb)
    acc = None
    for j in reversed(range(len(bp))):
        lhs = [ap[i] for i in reversed(range(len(ap))) if i + j < n]
        if dims == TN or len(lhs) == 1:
            terms = [lax.dot_general(x, bp[j], dims, preferred_element_type=F32) for x in lhs]
        else:
            m = a.shape[0]
            t = lax.dot_general(jnp.concatenate(lhs, axis=0), bp[j], dims, preferred_element_type=F32)
            terms = [t[k * m:(k + 1) * m] for k in range(len(lhs))]
        for t in terms:
            acc = t if acc is None else acc + t
    return acc


def _dotp(a, b, dims, level):
    return _dot(a, b, dims, pa=level, pb=level)


def _softplus2(z2):
    return jnp.maximum(z2, 0.0) + jnp.log(1.0 + jnp.exp2(-jnp.abs(z2))) * LOG2E


def _silu(g):
    return g / (1.0 + jnp.exp(-g))


def _rms(x, gain):
    return x * lax.rsqrt(jnp.mean(x * x, axis=-1, keepdims=True) + RMS_EPS) * gain


def _params(*sem):
    return pltpu.CompilerParams(dimension_semantics=sem, vmem_limit_bytes=VMEM_LIMIT)


def _token_shift(x_ref, xh_ref, sh_ref, gain, i, *, bm, seq, per_row_start):
    h = _rms(x_ref[...], gain)
    rolled = pltpu.roll(h, 1, 0)
    row = lax.broadcasted_iota(jnp.int32, (bm, 1), 0)
    if per_row_start:
        h_prev = jnp.where((row & (seq - 1)) == 0, sh_ref[...], rolled)
    else:
        halo = _rms(xh_ref[0], gain)
        first = jnp.where((i * bm) % seq == 0, sh_ref[...], halo[SUBLANES - 1:SUBLANES])
        h_prev = jnp.where(row == 0, first, rolled)
    return h, h_prev - h


def _shift_operands(x, shift, seq, bm, per_row_start, tile_index):
    m, d = x.shape
    nb = m // seq
    if per_row_start:
        sh = jnp.repeat(shift, seq, axis=0)
        sh_spec = pl.BlockSpec((bm, d), lambda *g: (tile_index(*g), 0))
    else:
        sh = shift.reshape(nb, 1, d)
        sh_spec = pl.BlockSpec((None, 1, d), lambda *g: ((tile_index(*g) * bm) // seq, 0, 0))
    xh = x.reshape(m // SUBLANES, SUBLANES, d)
    specs = [pl.BlockSpec((bm, d), lambda *g: (tile_index(*g), 0)),
             pl.BlockSpec((1, SUBLANES, d),
                          lambda *g: (jnp.maximum(tile_index(*g) * (bm // SUBLANES) - 1, 0), 0, 0)),
             sh_spec]
    return [x, xh, sh], specs


def _nsl_kernel(x_ref, xh_ref, sh_ref, gain_ref, mu_ref, w1_ref, w2_ref, w0_ref,
                a1_ref, a2_ref, a0_ref, h_ref, lw_ref, a_ref, *, bm, seq, per_row_start):
    h, dx = _token_shift(x_ref, xh_ref, sh_ref, gain_ref[...], pl.program_id(0),
                         bm=bm, seq=seq, per_row_start=per_row_start)
    h_ref[...] = h if per_row_start else h[bm - SUBLANES:]
    x4 = (h + dx * mu_ref[4:5, :]).astype(BF16)
    t = jnp.tanh(jnp.dot(x4, w1_ref[...], preferred_element_type=F32)).astype(BF16)
    wl = jnp.dot(t, w2_ref[...], preferred_element_type=F32) + w0_ref[...]
    lw_ref[...] = -math.exp(-0.5) / (1.0 + jnp.exp(-wl))
    x5 = (h + dx * mu_ref[5:6, :]).astype(BF16)
    u = jnp.dot(x5, a1_ref[...], preferred_element_type=F32).astype(BF16)
    al = jnp.dot(u, a2_ref[...], preferred_element_type=F32) + a0_ref[...]
    a_ref[...] = 1.0 / (1.0 + jnp.exp(-al))


def _shift_tiling(m, seq, row_tile):
    per_row_start = seq < SUBLANES * 2
    bm = m if per_row_start else min(row_tile, seq)
    assert m % bm == 0 and (per_row_start or seq % bm == 0)
    return per_row_start, bm


def _norm_shift_lora(x, shift, seq, gain, mu, w1, w2, w0, a1, a2, a0):
    m, d = x.shape
    per_row_start, bm = _shift_tiling(m, seq, ROW_TILE)
    nb = m // seq
    operands, specs = _shift_operands(x, shift, seq, bm, per_row_start, lambda i: i)
    row = lambda i: (i, 0)
    const = lambda i: (0, 0)
    kern = functools.partial(_nsl_kernel, bm=bm, seq=seq, per_row_start=per_row_start)
    if per_row_start:
        h_spec, h_shape = pl.BlockSpec((bm, d), row), (m, d)
    else:
        h_spec, h_shape = pl.BlockSpec((None, SUBLANES, d), lambda i: (i, 0, 0)), (m // bm, SUBLANES, d)
    h_tail, lw, a = pl.pallas_call(
        kern,
        grid=(m // bm,),
        in_specs=specs + [
            pl.BlockSpec((1, d), const),
            pl.BlockSpec(mu.shape, const),
            pl.BlockSpec(w1.shape, const),
            pl.BlockSpec(w2.shape, const),
            pl.BlockSpec((1, d), const),
            pl.BlockSpec(a1.shape, const),
            pl.BlockSpec(a2.shape, const),
            pl.BlockSpec((1, d), const),
        ],
        out_specs=[
            h_spec,
            pl.BlockSpec((bm, d), row),
            pl.BlockSpec((bm, d), row),
        ],
        out_shape=[
            jax.ShapeDtypeStruct(h_shape, F32),
            jax.ShapeDtypeStruct((m, d), F32),
            jax.ShapeDtypeStruct((m, d), F32),
        ],
        compiler_params=_params("arbitrary"),
        name="norm_shift_lora",
    )(*operands, gain.reshape(1, d), mu, w1, w2, w0.reshape(1, d), a1, a2, a0.reshape(1, d))
    if per_row_start:
        last = h_tail.reshape(nb, seq, d)[:, -1]
    else:
        last = h_tail.reshape(nb, seq // bm, SUBLANES, d)[:, -1, -1]
    return last, lw, a


def _proj_kernel(x_ref, xh_ref, sh_ref, gain_ref, mu_ref, w_ref, o_ref, *wb_ref, bm, seq, per_row_start):
    p = pl.program_id(0)
    t = pl.program_id(1)
    if wb_ref:
        @pl.when(t == 0)
        def _():
            wb_ref[0][...] = w_ref[...].astype(BF16)
        w = wb_ref[0][...]
        tile = t
    else:
        w = w_ref[...].astype(BF16)
        tile = 0
    h, dx = _token_shift(x_ref, xh_ref, sh_ref, gain_ref[...], tile, bm=bm, seq=seq, per_row_start=per_row_start)
    xs = (h + dx * mu_ref[pl.ds(p, 1), :]).astype(BF16)
    o_ref[...] = jnp.dot(xs, w, preferred_element_type=F32)


def _proj(x, shift, seq, gain, mu, w):
    m, d = x.shape
    np_, k, n = w.shape
    per_row_start, bm = _shift_tiling(m, seq, ROW_TILE)
    kern = functools.partial(_proj_kernel, bm=bm, seq=seq, per_row_start=per_row_start)
    const = lambda p, t: (0, 0)
    if m == bm and n % COL_TILE == 0:
        operands, specs = _shift_operands(x, shift, seq, bm, per_row_start, lambda p, j: 0)
        grid = (np_, n // COL_TILE)
        w_spec = pl.BlockSpec((None, k, COL_TILE), lambda p, j: (p, 0, j))
        o_spec = pl.BlockSpec((None, m, COL_TILE), lambda p, j: (p, 0, j))
        scratch = []
    else:
        operands, specs = _shift_operands(x, shift, seq, bm, per_row_start, lambda p, i: i)
        grid = (np_, m // bm)
        w_spec = pl.BlockSpec((None, k, n), lambda p, i: (p, 0, 0), pipeline_mode=pl.Buffered(1))
        o_spec = pl.BlockSpec((None, bm, n), lambda p, i: (p, i, 0))
        scratch = [pltpu.VMEM((k, n), BF16)]
    return pl.pallas_call(
        kern,
        grid=grid,
        in_specs=specs + [pl.BlockSpec((1, d), const), pl.BlockSpec(mu.shape, const), w_spec],
        out_specs=o_spec,
        out_shape=jax.ShapeDtypeStruct((np_, m, n), F32),
        scratch_shapes=scratch,
        compiler_params=_params("arbitrary", "arbitrary"),
        name="proj",
    )(*operands, gain.reshape(1, d), mu, w)


def _wkv_kernel(r_ref, k_ref, v_ref, g_ref, w_ref, a_ref, kk_ref, ka_ref, rk_ref, lg_ref, lb_ref,
                s0_ref, o_ref, so_ref, y_ref, *, chunk, rows, pairs, sub):
    c = pl.program_id(2)
    nc = pl.num_programs(2)
    C = chunk
    C2 = 2 * C

    @pl.when(c == 0)
    def _():
        zero = jnp.zeros((HEAD_A, HEAD_A), F32)
        for p in range(pairs):
            top = jnp.concatenate([s0_ref[2 * p], zero], axis=1)
            bot = jnp.concatenate([zero, s0_ref[2 * p + 1]], axis=1)
            y_ref[p] = jnp.concatenate([top, bot], axis=0)

    lane = lax.broadcasted_iota(jnp.int32, (1, LANES), 1)
    head0 = lane < HEAD_A
    ri = lax.broadcasted_iota(jnp.int32, (C2, C2), 0)
    ci = lax.broadcasted_iota(jnp.int32, (C2, C2), 1)
    same_head = (lax.broadcasted_iota(jnp.int32, (LANES, LANES), 0) >> HEAD_A_SHIFT) == (
        lax.broadcasted_iota(jnp.int32, (LANES, LANES), 1) >> HEAD_A_SHIFT)
    si = ri & (C - 1)
    sj = ci & (C - 1)
    strict = sj < si
    incl = sj <= si
    eye = jnp.where(ri == ci, 1.0, 0.0)
    trow = lax.broadcasted_iota(jnp.int32, (C, 1), 0)

    def stack(x):
        return jnp.concatenate([jnp.where(head0, x, 0.0), jnp.where(head0, 0.0, x)], axis=0)

    def pack(x):
        return x[:C] + x[C:]

    def head_sum(x):
        s0 = jnp.sum(jnp.where(head0, x, 0.0), axis=1, keepdims=True)
        s1 = jnp.sum(jnp.where(head0, 0.0, x), axis=1, keepdims=True)
        return jnp.where(head0, s0, s1)

    def pair_chain(p, tok):
        ln = slice(p * LANES, (p + 1) * LANES)

        def load(ref):
            x = ref[tok, ln]
            if rows < C:
                x = jnp.concatenate([x, jnp.zeros((C - rows, LANES), F32)], axis=0)
            return x

        R, K, V, Wl, A = (load(ref) for ref in (r_ref, k_ref, v_ref, w_ref, a_ref))

        kkp = K * kk_ref[:, ln]
        kk = kkp / jnp.maximum(jnp.sqrt(head_sum(kkp * kkp)), 1e-12)
        Km = K * (1.0 + (A - 1.0) * ka_ref[:, ln])

        Lc = Wl
        shift = 1
        while shift < C:
            Lc = Lc + jnp.where(trow >= shift, pltpu.roll(Lc, shift, 0), 0.0)
            shift *= 2
        yield
        Lend = Lc[C - 1:C, :]
        g_inv = jnp.exp(-Lc)
        g_rem = jnp.exp(Lend - Lc)
        kka = kk * A
        Ah = -kk * jnp.exp(Lc - Wl)
        Rh = R * jnp.exp(Lc)
        Bc = kka * g_inv
        Kc = Km * g_inv
        Bt = kka * g_rem
        Kt = Km * g_rem

        sAh = stack(Ah)
        lhs = jnp.concatenate([sAh, stack(Rh)], axis=0)
        rhs = jnp.concatenate([stack(Bc), stack(Kc)], axis=0)
        mx = _dotp(lhs, rhs, NT, WKV_P_MX)
        yield
        Nab =jnp.where(strict, mx[:C2, :C2], 0.0)
        nrest = jnp.concatenate([jnp.where(strict, mx[:C2, C2:], 0.0),
                                 jnp.where(incl, mx[C2:, C2:], 0.0)], axis=0)
        Nrb = jnp.where(incl, mx[C2:, :C2], 0.0)

        Tm = eye + Nab
        Pw = Nab
        nv = _dotp(nrest, stack(V), NN, WKV_P_APPLY)
        yield
        nsq = int(math.log2(C)) - 1
        Pw = _dotp(Pw, Pw, NN, WKV_P_INV)
        yield
        for it in range(nsq):
            if it < nsq - 1:
                both = _dotp(jnp.concatenate([Pw, Tm], axis=0), Pw, NN, WKV_P_INV)
                Pw, Tm = both[:C2], Tm + both[C2:]
            else:
                Tm = Tm + _dotp(Tm, Pw, NN, WKV_P_INV)
            yield

        tx = _dotp(Tm, jnp.concatenate([sAh, nv[:C2]], axis=1), NN, WKV_P_APPLY)
        yield

        Y = y_ref[p]
        sy = _dotp(jnp.concatenate([tx[:, :LANES], Rh], axis=0), Y, NT, WKV_P_STATE)
        yield
        Ub = sy[:C2] + tx[:, LANES:]
        U = pack(Ub)
        O = sy[C2:] + pack(_dotp(Nrb, Ub, NN, WKV_P_STATE) + nv[C2:])
        upd = _dotp(jnp.concatenate([U, V], axis=0), jnp.concatenate([Bt, Kt], axis=0), TN, WKV_P_STATE)
        yield
        y_new = Y * jnp.exp(Lend) + jnp.where(same_head, upd, 0.0)

        mean = head_sum(O) * (1.0 / HEAD_A)
        dlt = O - mean
        var = head_sum(dlt * dlt) * (1.0 / HEAD_A)
        on = dlt * lax.rsqrt(var + LNX_EPS) * lg_ref[:, ln] + lb_ref[:, ln]
        bonus = head_sum(R * Km * rk_ref[:, ln]) * V
        res = (on + bonus) * _silu(load(g_ref))
        yield
        y_ref[p] = y_new
        o_ref[tok, ln] = res[:rows].astype(o_ref.dtype)

    def chunk_step(s, carry):
        tok = pl.ds(pl.multiple_of(s * rows, rows), rows)
        chains = [pair_chain(p, tok) for p in range(pairs)]
        while chains:
            chains = [ch for ch in chains if next(ch, True) is None]
        return carry

    lax.fori_loop(0, sub, chunk_step, 0)

    @pl.when(c == nc - 1)
    def _():
        for p in range(pairs):
            y = y_ref[p]
            so_ref[2 * p] = y[:HEAD_A, :HEAD_A]
            so_ref[2 * p + 1] = y[HEAD_A:, HEAD_A:]


def _wkv(rkvg, lw, a, s0, seq, k_k, k_a, r_k, lnx_g, lnx_b):
    _, m, d = rkvg.shape
    nb = m // seq
    npair = d // LANES
    pairs = math.gcd(npair, WKV_PAIRS)
    width = pairs * LANES
    rows = min(WKV_CHUNK, seq)
    assert seq % rows == 0 and rows % SUBLANES == 0
    sub = math.gcd(seq // rows, WKV_SUBCHUNKS)
    blk = sub * rows
    nc = seq // blk
    tok = lambda b, h, c: (b * nc + c, h)
    par = lambda b, h, c: (0, h)
    st = lambda b, h, c: (b, h, 0, 0)

    def proj(p):
        return pl.BlockSpec((None, blk, width), lambda b, h, c: (p, b * nc + c, h))

    vec = lambda x: x.reshape(1, d)
    kern = functools.partial(_wkv_kernel, chunk=WKV_CHUNK, rows=rows, pairs=pairs, sub=sub)
    return pl.pallas_call(
        kern,
        grid=(nb, npair // pairs, nc),
        in_specs=[proj(0), proj(1), proj(2), proj(3),
                  pl.BlockSpec((blk, width), tok), pl.BlockSpec((blk, width), tok),
                  pl.BlockSpec((1, width), par), pl.BlockSpec((1, width), par),
                  pl.BlockSpec((1, width), par), pl.BlockSpec((1, width), par),
                  pl.BlockSpec((1, width), par),
                  pl.BlockSpec((None, 2 * pairs, HEAD_A, HEAD_A), st)],
        out_specs=[pl.BlockSpec((blk, width), tok),
                   pl.BlockSpec((None, 2 * pairs, HEAD_A, HEAD_A), st)],
        out_shape=[jax.ShapeDtypeStruct((m, d), BF16 if rows % (2 * SUBLANES) == 0 else F32),
                   jax.ShapeDtypeStruct((nb, 2 * npair, HEAD_A, HEAD_A), F32)],
        scratch_shapes=[pltpu.VMEM((pairs, LANES, LANES), F32)],
        compiler_params=_params("arbitrary", "arbitrary", "arbitrary"),
        name="wkv",
    )(rkvg, rkvg, rkvg, rkvg, lw, a, vec(k_k), vec(k_a), vec(r_k), vec(lnx_g), vec(lnx_b), s0)


def _cast_weight_once(w_ref, wb_ref):
    @pl.when(pl.program_id(0) == 0)
    def _():
        wb_ref[...] = w_ref[...].astype(BF16)


def _weight_spec(k, n, col):
    return pl.BlockSpec((k, n), lambda i: (0, col), pipeline_mode=pl.Buffered(1))


def _mm_res_kernel(x_ref, w_ref, r_ref, g_ref, o_ref, wb_ref, *, final_norm):
    _cast_weight_once(w_ref, wb_ref)
    y = r_ref[...] + jnp.dot(x_ref[...].astype(BF16), wb_ref[...], preferred_element_type=F32)
    if final_norm:
        y = _rms(y, g_ref[...])
    o_ref[...] = y


def _mm_res_cols_kernel(x_ref, w_ref, r_ref, g_ref, o_ref, *, final_norm):
    j = pl.program_id(0)
    cols = pl.ds(pl.multiple_of(j * COL_TILE, COL_TILE), COL_TILE)
    o_ref[:, cols] = r_ref[:, cols] + jnp.dot(x_ref[...].astype(BF16), w_ref[...].astype(BF16),
                                              preferred_element_type=F32)
    if final_norm:
        @pl.when(j == pl.num_programs(0) - 1)
        def _():
            o_ref[...] = _rms(o_ref[...], g_ref[...])


def _mm_res(x, w, res, gain=None):
    m, k = x.shape
    n = w.shape[1]
    bm = min(ROW_TILE, m)
    assert m % bm == 0
    final_norm = gain is not None
    g = (gain if final_norm else jnp.ones((n,), F32)).reshape(1, n)
    if m == bm and n % COL_TILE == 0:
        whole = lambda j: (0, 0)
        return pl.pallas_call(
            functools.partial(_mm_res_cols_kernel, final_norm=final_norm),
            grid=(n // COL_TILE,),
            in_specs=[pl.BlockSpec((m, k), whole),
                      pl.BlockSpec((k, COL_TILE), lambda j: (0, j)),
                      pl.BlockSpec((m, n), whole),
                      pl.BlockSpec((1, n), whole)],
            out_specs=pl.BlockSpec((m, n), whole),
            out_shape=jax.ShapeDtypeStruct((m, n), F32),
            compiler_params=_params("arbitrary"),
            name="mm_res_cols",
        )(x, w, res, g)
    return pl.pallas_call(
        functools.partial(_mm_res_kernel, final_norm=final_norm),
        grid=(m // bm,),
        in_specs=[pl.BlockSpec((bm, k), lambda i: (i, 0)),
                  _weight_spec(k, n, 0),
                  pl.BlockSpec((bm, n), lambda i: (i, 0)),
                  pl.BlockSpec((1, n), lambda i: (0, 0))],
        out_specs=pl.BlockSpec((bm, n), lambda i: (i, 0)),
        out_shape=jax.ShapeDtypeStruct((m, n), F32),
        scratch_shapes=[pltpu.VMEM((k, n), BF16)],
        compiler_params=_params("arbitrary"),
        name="mm_res",
    )(x, w, res, g)


def _norm_mm_kernel(x_ref, g_ref, w_ref, *refs, resident_weight, n_out):
    outs = refs[:n_out]
    if resident_weight:
        wb_ref = refs[n_out]
        _cast_weight_once(w_ref, wb_ref)
        w = wb_ref[...]
    else:
        w = w_ref[...].astype(BF16)
    xn = _rms(x_ref[...], g_ref[...]).astype(BF16)
    y = jnp.dot(xn, w, preferred_element_type=F32)
    for o_ref in outs:
        o_ref[...] = y.astype(o_ref.dtype)


def _norm_mm(x, gain, w, col, n, with_bf16=False):
    m, k = x.shape
    bm = min(ROW_TILE, m)
    assert m % bm == 0 and w.shape[1] % n == 0
    dtypes = [F32, BF16] if with_bf16 else [F32]
    out_shape = [jax.ShapeDtypeStruct((m, n), dt) for dt in dtypes]
    if m == bm and n % COL_TILE == 0:
        nj = n // COL_TILE
        outs = pl.pallas_call(
            functools.partial(_norm_mm_kernel, resident_weight=False, n_out=len(dtypes)),
            grid=(nj,),
            in_specs=[pl.BlockSpec((m, k), lambda j: (0, 0)),
                      pl.BlockSpec((1, k), lambda j: (0, 0)),
                      pl.BlockSpec((k, COL_TILE), lambda j: (0, col * nj + j))],
            out_specs=[pl.BlockSpec((m, COL_TILE), lambda j: (0, j)) for _ in dtypes],
            out_shape=out_shape,
            compiler_params=_params("arbitrary"),
            name="norm_mm_cols",
        )(x, gain.reshape(1, k), w)
    else:
        outs = pl.pallas_call(
            functools.partial(_norm_mm_kernel, resident_weight=True, n_out=len(dtypes)),
            grid=(m // bm,),
            in_specs=[pl.BlockSpec((bm, k), lambda i: (i, 0)),
                      pl.BlockSpec((1, k), lambda i: (0, 0)),
                      _weight_spec(k, n, col)],
            out_specs=[pl.BlockSpec((bm, n), lambda i: (i, 0)) for _ in dtypes],
            out_shape=out_shape,
            scratch_shapes=[pltpu.VMEM((k, n), BF16)],
            compiler_params=_params("arbitrary"),
            name="norm_mm",
        )(x, gain.reshape(1, k), w)
    return tuple(outs) if with_bf16 else outs[0]


SB_BLOCK = 256
SB_HEADS = 8
SB_CUM_PIECES = 1
SBS_PAGES = 8


def _sbp_kernel(bias_ref, q_ref, kb_ref, vb_ref, g_ref, o_ref, *, blk, scale, heads):
    hg = pl.program_id(1)
    qi = pl.program_id(2)
    lanes = [slice(h * HEAD_B, (h + 1) * HEAD_B) for h in range(heads)]
    biases = [bias_ref[hg * heads + h] * LOG2E for h in range(heads)]
    qs = [(q_ref[:, ln] * (scale * LOG2E)).astype(BF16) for ln in lanes]
    def suffix_ones(n):
        return jnp.where(lax.broadcasted_iota(jnp.int32, (n, n), 0) > lax.broadcasted_iota(jnp.int32, (n, n), 1),
                         1.0, 0.0).astype(BF16)

    def tile_chain(h, q, keys, carry, mask, suffix, tot, val):
        ln = lanes[h]
        z = lax.dot_general(q, kb_ref[keys, ln], NT, preferred_element_type=F32) + biases[h]
        yield
        sp = _softplus2(z)
        lk = -sp if mask is None else jnp.where(mask, -sp, 0.0)
        log_beta = z - sp
        tot[h] = jnp.sum(lk, axis=1, keepdims=True)
        yield
        rs = _dot(lk, suffix, pa=SB_CUM_PIECES)
        yield
        e = log_beta + rs
        att = jnp.exp2(e if carry is None else e + carry())
        if mask is not None:
            att = jnp.where(mask, att, 0.0)
        yield
        val[h] = jnp.dot(att.astype(BF16), vb_ref[keys, ln], preferred_element_type=F32)

    def run(chains):
        while chains:
            chains = [ch for ch in chains if next(ch, True) is None]

    suffix = suffix_ones(blk)

    causal = lax.broadcasted_iota(jnp.int32, (blk, blk), 1) < lax.broadcasted_iota(jnp.int32, (blk, blk), 0)
    carries, accs = [None] * heads, [None] * heads
    diag_keys = pl.ds(pl.multiple_of(qi * blk, blk), blk)
    run([tile_chain(h, qs[h], diag_keys, None, causal, suffix, carries, accs) for h in range(heads)])

    def block(j, carries, accs):
        keys = pl.ds(pl.multiple_of(j * blk, blk), blk)
        tot, val = [None] * heads, [None] * heads
        run([tile_chain(h, qs[h], keys, functools.partial(carries.__getitem__, h), None, suffix, tot, val)
             for h in range(heads)])
        return [c + t for c, t in zip(carries, tot)], [a + v for a, v in zip(accs, val)]

    def body(it, ca):
        cs, as_ = block(qi - 1 - it, list(ca[:heads]), list(ca[heads:]))
        return tuple(cs) + tuple(as_)

    ca = lax.fori_loop(0, qi, body, tuple(carries) + tuple(accs))
    for h, ln in enumerate(lanes):
        o_ref[:, ln] = (ca[heads + h] * _silu(g_ref[:, ln])).astype(o_ref.dtype)


def _sb_prompt(q, k, v, g, bias, seq):
    m, d = q.shape
    nb = m // seq
    nh = d // HEAD_B
    heads = math.gcd(nh, SB_HEADS)
    width = heads * HEAD_B
    blk = min(SB_BLOCK, seq)
    assert seq % blk == 0
    nq = seq // blk
    qmap = lambda b, h, i: (b * nq + i, h)
    kmap = lambda b, h, i: (b, h)
    kern = functools.partial(_sbp_kernel, blk=blk, scale=1.0 / math.sqrt(HEAD_B), heads=heads)
    return pl.pallas_call(
        kern,
        grid=(nb, nh // heads, nq),
        in_specs=[pl.BlockSpec(memory_space=pltpu.SMEM),
                  pl.BlockSpec((blk, width), qmap),
                  pl.BlockSpec((seq, width), kmap),
                  pl.BlockSpec((seq, width), kmap),
                  pl.BlockSpec((blk, width), qmap)],
        out_specs=pl.BlockSpec((blk, width), qmap),
        out_shape=jax.ShapeDtypeStruct((m, d), BF16),
        compiler_params=_params("arbitrary", "arbitrary", "arbitrary"),
        name="sb_prompt",
    )(bias, q, k, v, g)


def _wide_page(refs, page):
    ht = refs[0].shape[1]
    rows = [r.reshape(page * ht, HEAD_B) for r in refs]
    return jnp.concatenate([r[pl.ds(j, page, stride=ht), :].astype(BF16) for r in rows for j in range(ht)],
                           axis=1)


def _sbs_kernel(pt_ref, qbd_ref, bias_ref, kn_ref, vn_ref, g_ref, *rest, ts, page, nh, nkv):
    del pt_ref
    k_refs, v_refs = rest[:nkv], rest[nkv:2 * nkv]
    o_ref, acc_ref, carry_ref = rest[2 * nkv:]
    ntile = nh // k_refs[0].shape[1]
    p = pl.program_id(1)
    last = pl.num_programs(1) - 1
    cols = nh * ts
    qbd = qbd_ref[...]
    bias = bias_ref[...]
    ri = lax.broadcasted_iota(jnp.int32, (page, page), 0)
    ci = lax.broadcasted_iota(jnp.int32, (page, page), 1)
    suffix = jnp.where(ci > ri, 1.0, 0.0).astype(BF16)

    wide = functools.partial(_wide_page, page=page)

    def process(ks, vs, mask):
        zs = [jnp.dot(k, qbd, preferred_element_type=F32) + bias for k in ks]
        sps = [_softplus2(z) for z in zs]
        lks = [-sp if mask is None else jnp.where(mask, -sp, 0.0) for sp in sps]
        rests = [_dot(suffix, lk, pb=SB_CUM_PIECES) for lk in lks]
        carry = carry_ref[...]
        atts = []
        for z, sp, lk, rs in zip(zs, sps, lks, rests):
            att = jnp.exp2(z - sp + rs + carry)
            if mask is not None:
                att = jnp.where(mask, att, 0.0)
            atts.append(att.astype(BF16))
            carry = carry + jnp.sum(lk, axis=0, keepdims=True)
        att = atts[0] if len(atts) == 1 else jnp.concatenate(atts, axis=0)
        val = vs[0] if len(vs) == 1 else jnp.concatenate(vs, axis=0)
        acc_ref[...] += lax.dot_general(att, val, TN, preferred_element_type=F32)
        carry_ref[...] = carry

    @pl.when(p == 0)
    def _():
        acc_ref[...] = jnp.zeros_like(acc_ref)
        carry_ref[...] = jnp.zeros_like(carry_ref)
        pad = jnp.zeros((page - ts, kn_ref.shape[-1]), BF16)
        kn = jnp.concatenate([kn_ref[...].astype(BF16), pad], axis=0)
        vn = jnp.concatenate([vn_ref[...].astype(BF16), pad], axis=0)
        krow = lax.broadcasted_iota(jnp.int32, (page, cols), 0)
        qcol = lax.broadcasted_iota(jnp.int32, (page, cols), 1) & (ts - 1)
        process([kn], [vn], krow < qcol)

    @pl.when(p > 0)
    def _():
        pages = range(0, nkv, ntile)
        process([wide(k_refs[j:j + ntile]) for j in pages], [wide(v_refs[j:j + ntile]) for j in pages], None)

    @pl.when(p == last)
    def _():
        g = g_ref[...]
        for h in range(nh):
            sl = slice(h * HEAD_B, (h + 1) * HEAD_B)
            o = acc_ref[h * ts:(h + 1) * ts, sl]
            o_ref[:, sl] = (o * _silu(g[:, sl])).astype(o_ref.dtype)


def _sb_sample(q, k_new, v_new, g, cache_k, cache_v, page_table, bias, ts):
    m, d = q.shape
    nb = m // ts
    nh = d // HEAD_B
    n_pool, page = cache_k.shape[:2]
    n_pages = page_table.shape[1]
    cols = nh * ts
    assert ts % SUBLANES == 0 and ts & (ts - 1) == 0
    scale2 = LOG2E / math.sqrt(HEAD_B)
    q4 = (q * scale2).reshape(nb, ts, nh, HEAD_B)
    eye = jnp.eye(nh, dtype=F32)
    qbd = jnp.einsum("bihc,hg->bhcgi", q4, eye).reshape(nb, d, cols).astype(BF16)
    bias_cols = jnp.repeat(bias * LOG2E, ts).reshape(1, cols)
    ht = math.gcd(nh, SUBLANES)
    ntile = nh // ht
    npg = math.gcd(n_pages, SBS_PAGES)
    nkv = npg * ntile
    tok = lambda b, p, pt: (b, 0, 0)

    def page_spec(j, t):
        return pl.BlockSpec((None, page, ht, HEAD_B),
                            lambda b, p, pt: (pt[b, n_pages - 1 - (jnp.maximum(p, 1) - 1) * npg - j], 0, t, 0))

    page_specs = [page_spec(j, t) for j in range(npg) for t in range(ntile)]
    grid_spec = pltpu.PrefetchScalarGridSpec(
        num_scalar_prefetch=1,
        grid=(nb, n_pages // npg + 1),
        in_specs=[pl.BlockSpec((None, d, cols), tok),
                  pl.BlockSpec((1, cols), lambda b, p, pt: (0, 0)),
                  pl.BlockSpec((None, ts, d), tok),
                  pl.BlockSpec((None, ts, d), tok),
                  pl.BlockSpec((None, ts, d), tok)] + page_specs + page_specs,
        out_specs=pl.BlockSpec((None, ts, d), tok),
        scratch_shapes=[pltpu.VMEM((cols, d), F32), pltpu.VMEM((1, cols), F32)],
    )
    kern = functools.partial(_sbs_kernel, ts=ts, page=page, nh=nh, nkv=nkv)
    out = pl.pallas_call(
        kern,
        grid_spec=grid_spec,
        out_shape=jax.ShapeDtypeStruct((nb, ts, d), F32),
        compiler_params=_params("arbitrary", "arbitrary"),
        name="sb_sample",
    )(page_table, qbd, bias_cols, k_new.reshape(nb, ts, d), v_new.reshape(nb, ts, d), g.reshape(nb, ts, d),
      *([cache_k] * nkv), *([cache_v] * nkv))
    return out.reshape(m, d)


def _pad_lora(w_in, w_out):
    r = w_in.shape[1]
    w_in = jnp.pad(w_in, ((0, 0), (0, LORA_PAD - r))).astype(BF16)
    w_out = jnp.pad(w_out, ((0, LORA_PAD - r), (0, 0))).astype(BF16)
    return w_in, w_out


def kernel(x_prompt, x_sample, state_shift, state_wkv, cache_k, cache_v, page_table, a_norm, a_mu, a_w_rkvg, a_w0, a_w1, a_w2, a_a0, a_a1, a_a2, a_k_k, a_k_a, a_r_k, a_lnx_g, a_lnx_b, a_w_o, kv_norm, w_kv, b_norm, b_w_qg, b_logit_bias, b_w_o, final_norm):
    nbp, tp, d = x_prompt.shape
    nbs, ts, _ = x_sample.shape
    n_a = a_norm.shape[0]
    n_b = b_norm.shape[0]
    nh_a = d // HEAD_A
    nh_b = d // HEAD_B

    hp = x_prompt.reshape(nbp * tp, d)
    hs = x_sample.reshape(nbs * ts, d)
    shift_p, wkv_p, shift_s, wkv_s = [], [], [], []

    for i in range(n_a):
        w1, w2 = _pad_lora(a_w1[i], a_w2[i])
        a1, a2 = _pad_lora(a_a1[i], a_a2[i])
        w_rkvg = a_w_rkvg[i]
        w_o = a_w_o[i]
        r_k = a_r_k[i].reshape(d)

        def a_layer(h_in, seq, shift0, s0):
            h_last, lw, a = _norm_shift_lora(h_in, shift0, seq, a_norm[i], a_mu[i], w1, w2, a_w0[i],
                                             a1, a2, a_a0[i])
            rkvg = _proj(h_in, shift0, seq, a_norm[i], a_mu[i], w_rkvg)
            og, s_out = _wkv(rkvg, lw, a, s0.astype(F32), seq,
                         a_k_k[i], a_k_a[i], r_k, a_lnx_g[i], a_lnx_b[i])
            h_out = _mm_res(og, w_o, h_in)
            return h_out, h_last, s_out.astype(s0.dtype)

        hp, sh, st = a_layer(hp, tp, jnp.zeros((nbp, d), F32),
                             jnp.zeros((nbp,) + state_wkv.shape[2:], state_wkv.dtype))
        shift_p.append(sh); wkv_p.append(st)
        hs, sh, st = a_layer(hs, ts, state_shift[i], state_wkv[i])
        shift_s.append(sh); wkv_s.append(st)

    k_p, k_p16 = _norm_mm(hp, kv_norm, w_kv, 0, d, with_bf16=True)
    v_p, v_p16 = _norm_mm(hp, kv_norm, w_kv, 1, d, with_bf16=True)
    k_s = _norm_mm(hs, kv_norm, w_kv, 0, d)
    v_s = _norm_mm(hs, kv_norm, w_kv, 1, d)

    for j in range(n_b):
        w_qg = b_w_qg[j]
        w_o = b_w_o[j]
        gain = final_norm if j == n_b - 1 else None
        qp = _norm_mm(hp, b_norm[j], w_qg, 0, d)
        gp = _norm_mm(hp, b_norm[j], w_qg, 1, d)
        op = _sb_prompt(qp, k_p16, v_p16, gp, b_logit_bias[j], tp)
        hp = _mm_res(op, w_o, hp, gain)
        qs = _norm_mm(hs, b_norm[j], w_qg, 0, d)
        gs = _norm_mm(hs, b_norm[j], w_qg, 1, d)
        os_ = _sb_sample(qs, k_s, v_s, gs, cache_k, cache_v, page_table, b_logit_bias[j], ts)
        hs = _mm_res(os_, w_o, hs, gain)

    if n_b == 0:
        raise NotImplementedError("trunk without stick-breaking layers")

    return (hp.reshape(nbp, tp, d), hs.reshape(nbs, ts, d),
            jnp.stack(shift_p), jnp.stack(wkv_p),
            k_p.reshape(nbp, tp, nh_b, HEAD_B), v_p.reshape(nbp, tp, nh_b, HEAD_B),
            jnp.stack(shift_s), jnp.stack(wkv_s),
            k_s.reshape(nbs, ts, nh_b, HEAD_B), v_s.reshape(nbs, ts, nh_b, HEAD_B))
```

```python
import functools
import math

import jax
import jax.numpy as jnp
from jax import lax
from jax.experimental import pallas as pl
from jax.experimental.pallas import tpu as pltpu

F32 = jnp.float32
BF16 = jnp.bfloat16

HEAD_A = 64
HEAD_A_SHIFT = 6
HEAD_B = 128
LANES = 128
SUBLANES = 8
LOG2E = 1.4426950408889634
RMS_EPS = 1e-6
LNX_EPS = 64e-5
LORA_PAD = 128
WKV_CHUNK = 64
WKV_SUBCHUNKS = 4
WKV_PAIRS = 16
WKV_P_MX = 1
WKV_P_INV = 1
WKV_P_APPLY = 1
WKV_P_STATE = 1
VMEM_LIMIT = 56 * 1024 * 1024
ROW_TILE = 512
COL_TILE = 512

NN = (((1,), (0,)), ((), ()))
NT = (((1,), (1,)), ((), ()))
TN = (((0,), (0,)), ((), ()))


def _split(x, n):
    parts = []
    for _ in range(n - 1):
        hi = x.astype(BF16)
        parts.append(hi)
        x = x - hi.astype(F32)
    parts.append(x.astype(BF16))
    return parts


def _dot(a, b, dims=NN, pa=1, pb=1):
    ap = _split(a, pa) if pa > 1 or a.dtype != BF16 else [a]
    bp = _split(b, pb) if pb > 1 or b.dtype != BF16 else [b]
    n = max(pa, pb)
    acc = None
    for j in reversed(range(len(bp))):
        lhs = [ap[i] for i in reversed(range(len(ap))) if i + j < n]
        if dims == TN or len(lhs) == 1:
            terms = [lax.dot_general(x, bp[j], dims, preferred_element_type=F32) for x in lhs]
        else:
            m = a.shape[0]
            t = lax.dot_general(jnp.concatenate(lhs, axis=0), bp[j], dims, preferred_element_type=F32)
            terms = [t[k * m:(k + 1) * m] for k in range(len(lhs))]
        for t in terms:
            acc = t if acc is None else acc + t
    return acc


def _dotp(a, b, dims, level):
    return _dot(a, b, dims, pa=level, pb=level)


def _softplus2(z2):
    return jnp.maximum(z2, 0.0) + jnp.log(1.0 + jnp.exp2(-jnp.abs(z2))) * LOG2E


def _silu(g):
    return g / (1.0 + jnp.exp(-g))


def _rms(x, gain):
    return x * lax.rsqrt(jnp.mean(x * x, axis=-1, keepdims=True) + RMS_EPS) * gain


def _params(*sem):
    return pltpu.CompilerParams(dimension_semantics=sem, vmem_limit_bytes=VMEM_LIMIT)


def _token_shift(x_ref, xh_ref, sh_ref, gain, i, *, bm, seq, per_row_start):
    h = _rms(x_ref[...], gain)
    rolled = pltpu.roll(h, 1, 0)
    row = lax.broadcasted_iota(jnp.int32, (bm, 1), 0)
    if per_row_start:
        h_prev = jnp.where((row & (seq - 1)) == 0, sh_ref[...], rolled)
    else:
        halo = _rms(xh_ref[0], gain)
        first = jnp.where((i * bm) % seq == 0, sh_ref[...], halo[SUBLANES - 1:SUBLANES])
        h_prev = jnp.where(row == 0, first, rolled)
    return h, h_prev - h


def _shift_operands(x, shift, seq, bm, per_row_start, tile_index):
    m, d = x.shape
    nb = m // seq
    if per_row_start:
        sh = jnp.repeat(shift, seq, axis=0)
        sh_spec = pl.BlockSpec((bm, d), lambda *g: (tile_index(*g), 0))
    else:
        sh = shift.reshape(nb, 1, d)
        sh_spec = pl.BlockSpec((None, 1, d), lambda *g: ((tile_index(*g) * bm) // seq, 0, 0))
    xh = x.reshape(m // SUBLANES, SUBLANES, d)
    specs = [pl.BlockSpec((bm, d), lambda *g: (tile_index(*g), 0)),
             pl.BlockSpec((1, SUBLANES, d),
                          lambda *g: (jnp.maximum(tile_index(*g) * (bm // SUBLANES) - 1, 0), 0, 0)),
             sh_spec]
    return [x, xh, sh], specs


def _nsl_kernel(x_ref, xh_ref, sh_ref, gain_ref, mu_ref, w1_ref, w2_ref, w0_ref,
                a1_ref, a2_ref, a0_ref, h_ref, dx_ref, lw_ref, a_ref, *, bm, seq, per_row_start):
    h, dx = _token_shift(x_ref, xh_ref, sh_ref, gain_ref[...], pl.program_id(0),
                         bm=bm, seq=seq, per_row_start=per_row_start)
    h_ref[...] = h
    dx_ref[...] = dx
    x4 = (h + dx * mu_ref[4:5, :]).astype(BF16)
    t = jnp.tanh(jnp.dot(x4, w1_ref[...], preferred_element_type=F32)).astype(BF16)
    wl = jnp.dot(t, w2_ref[...], preferred_element_type=F32) + w0_ref[...]
    lw_ref[...] = -math.exp(-0.5) / (1.0 + jnp.exp(-wl))
    x5 = (h + dx * mu_ref[5:6, :]).astype(BF16)
    u = jnp.dot(x5, a1_ref[...], preferred_element_type=F32).astype(BF16)
    al = jnp.dot(u, a2_ref[...], preferred_element_type=F32) + a0_ref[...]
    a_ref[...] = 1.0 / (1.0 + jnp.exp(-al))


def _shift_tiling(m, seq, row_tile):
    per_row_start = seq < SUBLANES * 2
    bm = m if per_row_start else min(row_tile, seq)
    assert m % bm == 0 and (per_row_start or seq % bm == 0)
    return per_row_start, bm


def _norm_shift_lora(x, shift, seq, gain, mu, w1, w2, w0, a1, a2, a0):
    m, d = x.shape
    per_row_start, bm = _shift_tiling(m, seq, ROW_TILE)
    operands, specs = _shift_operands(x, shift, seq, bm, per_row_start, lambda i: i)
    row = lambda i: (i, 0)
    const = lambda i: (0, 0)
    kern = functools.partial(_nsl_kernel, bm=bm, seq=seq, per_row_start=per_row_start)
    return pl.pallas_call(
        kern,
        grid=(m // bm,),
        in_specs=specs + [
            pl.BlockSpec((1, d), const),
            pl.BlockSpec(mu.shape, const),
            pl.BlockSpec(w1.shape, const),
            pl.BlockSpec(w2.shape, const),
            pl.BlockSpec((1, d), const),
            pl.BlockSpec(a1.shape, const),
            pl.BlockSpec(a2.shape, const),
            pl.BlockSpec((1, d), const),
        ],
        out_specs=[pl.BlockSpec((bm, d), row)] * 4,
        out_shape=[jax.ShapeDtypeStruct((m, d), F32)] * 4,
        compiler_params=_params("arbitrary"),
        name="norm_shift_lora",
    )(*operands, gain.reshape(1, d), mu, w1, w2, w0.reshape(1, d), a1, a2, a0.reshape(1, d))


def _proj_kernel(h_ref, dx_ref, mu_ref, w_ref, o_ref, *wb_ref):
    p = pl.program_id(0)
    if wb_ref:
        @pl.when(pl.program_id(1) == 0)
        def _():
            wb_ref[0][...] = w_ref[...].astype(BF16)
        w = wb_ref[0][...]
    else:
        w = w_ref[...].astype(BF16)
    xs = (h_ref[...] + dx_ref[...] * mu_ref[pl.ds(p, 1), :]).astype(BF16)
    o_ref[...] = jnp.dot(xs, w, preferred_element_type=F32)


def _proj(h, dx, mu, w):
    m, d = h.shape
    np_, k, n = w.shape
    bm = min(ROW_TILE, m)
    assert m % bm == 0
    if m == bm and n % COL_TILE == 0:
        grid = (np_, n // COL_TILE)
        x_spec = pl.BlockSpec((m, d), lambda p, j: (0, 0))
        w_spec = pl.BlockSpec((None, k, COL_TILE), lambda p, j: (p, 0, j))
        o_spec = pl.BlockSpec((None, m, COL_TILE), lambda p, j: (p, 0, j))
        scratch = []
    else:
        grid = (np_, m // bm)
        x_spec = pl.BlockSpec((bm, d), lambda p, i: (i, 0))
        w_spec = pl.BlockSpec((None, k, n), lambda p, i: (p, 0, 0), pipeline_mode=pl.Buffered(1))
        o_spec = pl.BlockSpec((None, bm, n), lambda p, i: (p, i, 0))
        scratch = [pltpu.VMEM((k, n), BF16)]
    return pl.pallas_call(
        _proj_kernel,
        grid=grid,
        in_specs=[x_spec, x_spec, pl.BlockSpec(mu.shape, lambda p, t: (0, 0)), w_spec],
        out_specs=o_spec,
        out_shape=jax.ShapeDtypeStruct((np_, m, n), F32),
        scratch_shapes=scratch,
        compiler_params=_params("arbitrary", "arbitrary"),
        name="proj",
    )(h, dx, mu, w)


def _wkv_kernel(r_ref, k_ref, v_ref, g_ref, w_ref, a_ref, kk_ref, ka_ref, rk_ref, lg_ref, lb_ref,
                s0_ref, o_ref, so_ref, y_ref, *, chunk, rows, pairs, sub):
    c = pl.program_id(2)
    nc = pl.num_programs(2)
    C = chunk
    C2 = 2 * C

    @pl.when(c == 0)
    def _():
        zero = jnp.zeros((HEAD_A, HEAD_A), F32)
        for p in range(pairs):
            top = jnp.concatenate([s0_ref[2 * p], zero], axis=1)
            bot = jnp.concatenate([zero, s0_ref[2 * p + 1]], axis=1)
            y_ref[p] = jnp.concatenate([top, bot], axis=0)

    lane = lax.broadcasted_iota(jnp.int32, (1, LANES), 1)
    head0 = lane < HEAD_A
    ri = lax.broadcasted_iota(jnp.int32, (C2, C2), 0)
    ci = lax.broadcasted_iota(jnp.int32, (C2, C2), 1)
    same_head = (lax.broadcasted_iota(jnp.int32, (LANES, LANES), 0) >> HEAD_A_SHIFT) == (
        lax.broadcasted_iota(jnp.int32, (LANES, LANES), 1) >> HEAD_A_SHIFT)
    si = ri & (C - 1)
    sj = ci & (C - 1)
    strict = sj < si
    incl = sj <= si
    eye = jnp.where(ri == ci, 1.0, 0.0)
    trow = lax.broadcasted_iota(jnp.int32, (C, 1), 0)

    def stack(x):
        return jnp.concatenate([jnp.where(head0, x, 0.0), jnp.where(head0, 0.0, x)], axis=0)

    def pack(x):
        return x[:C] + x[C:]

    def head_sum(x):
        s0 = jnp.sum(jnp.where(head0, x, 0.0), axis=1, keepdims=True)
        s1 = jnp.sum(jnp.where(head0, 0.0, x), axis=1, keepdims=True)
        return jnp.where(head0, s0, s1)

    def pair_chain(p, tok):
        ln = slice(p * LANES, (p + 1) * LANES)

        def load(ref):
            x = ref[tok, ln]
            if rows < C:
                x = jnp.concatenate([x, jnp.zeros((C - rows, LANES), F32)], axis=0)
            return x

        R, K, V, Wl, A = (load(ref) for ref in (r_ref, k_ref, v_ref, w_ref, a_ref))

        kkp = K * kk_ref[:, ln]
        kk = kkp / jnp.maximum(jnp.sqrt(head_sum(kkp * kkp)), 1e-12)
        Km = K * (1.0 + (A - 1.0) * ka_ref[:, ln])

        Lc = Wl
        shift = 1
        while shift < C:
            Lc = Lc + jnp.where(trow >= shift, pltpu.roll(Lc, shift, 0), 0.0)
            shift *= 2
        yield
        Lend = Lc[C - 1:C, :]
        g_inv = jnp.exp(-Lc)
        g_rem = jnp.exp(Lend - Lc)
        kka = kk * A
        Ah = -kk * jnp.exp(Lc - Wl)
        Rh = R * jnp.exp(Lc)
        Bc = kka * g_inv
        Kc = Km * g_inv
        Bt = kka * g_rem
        Kt = Km * g_rem

        sAh = stack(Ah)
        lhs = jnp.concatenate([sAh, stack(Rh)], axis=0)
        rhs = jnp.concatenate([stack(Bc), stack(Kc)], axis=0)
        mx = _dotp(lhs, rhs, NT, WKV_P_MX)
        yield
        Nab =jnp.where(strict, mx[:C2, :C2], 0.0)
        nrest = jnp.concatenate([jnp.where(strict, mx[:C2, C2:], 0.0),
                                 jnp.where(incl, mx[C2:, C2:], 0.0)], axis=0)
        Nrb = jnp.where(incl, mx[C2:, :C2], 0.0)

        Tm = eye + Nab
        Pw = Nab
        nv = _dotp(nrest, stack(V), NN, WKV_P_APPLY)
        yield
        nsq = int(math.log2(C)) - 1
        Pw = _dotp(Pw, Pw, NN, WKV_P_INV)
        yield
        for it in range(nsq):
            if it < nsq - 1:
                both = _dotp(jnp.concatenate([Pw, Tm], axis=0), Pw, NN, WKV_P_INV)
                Pw, Tm = both[:C2], Tm + both[C2:]
            else:
                Tm = Tm + _dotp(Tm, Pw, NN, WKV_P_INV)
            yield

        tx = _dotp(Tm, jnp.concatenate([sAh, nv[:C2]], axis=1), NN, WKV_P_APPLY)
        yield

        Y = y_ref[p]
        sy = _dotp(jnp.concatenate([tx[:, :LANES], Rh], axis=0), Y, NT, WKV_P_STATE)
        yield
        Ub = sy[:C2] + tx[:, LANES:]
        U = pack(Ub)
        O = sy[C2:] + pack(_dotp(Nrb, Ub, NN, WKV_P_STATE) + nv[C2:])
        upd = _dotp(jnp.concatenate([U, V], axis=0), jnp.concatenate([Bt, Kt], axis=0), TN, WKV_P_STATE)
        yield
        y_new = Y * jnp.exp(Lend) + jnp.where(same_head, upd, 0.0)

        mean = head_sum(O) * (1.0 / HEAD_A)
        dlt = O - mean
        var = head_sum(dlt * dlt) * (1.0 / HEAD_A)
        on = dlt * lax.rsqrt(var + LNX_EPS) * lg_ref[:, ln] + lb_ref[:, ln]
        bonus = head_sum(R * Km * rk_ref[:, ln]) * V
        res = (on + bonus) * _silu(load(g_ref))
        yield
        y_ref[p] = y_new
        o_ref[tok, ln] = res[:rows].astype(o_ref.dtype)

    def chunk_step(s, carry):
        tok = pl.ds(pl.multiple_of(s * rows, rows), rows)
        chains = [pair_chain(p, tok) for p in range(pairs)]
        while chains:
            chains = [ch for ch in chains if next(ch, True) is None]
        return carry

    lax.fori_loop(0, sub, chunk_step, 0)

    @pl.when(c == nc - 1)
    def _():
        for p in range(pairs):
            y = y_ref[p]
            so_ref[2 * p] = y[:HEAD_A, :HEAD_A]
            so_ref[2 * p + 1] = y[HEAD_A:, HEAD_A:]


def _wkv(rkvg, lw, a, s0, seq, k_k, k_a, r_k, lnx_g, lnx_b):
    _, m, d = rkvg.shape
    nb = m // seq
    npair = d // LANES
    pairs = math.gcd(npair, WKV_PAIRS)
    width = pairs * LANES
    rows = min(WKV_CHUNK, seq)
    assert seq % rows == 0 and rows % SUBLANES == 0
    sub = math.gcd(seq // rows, WKV_SUBCHUNKS)
    blk = sub * rows
    nc = seq // blk
    tok = lambda b, h, c: (b * nc + c, h)
    par = lambda b, h, c: (0, h)
    st = lambda b, h, c: (b, h, 0, 0)

    def proj(p):
        return pl.BlockSpec((None, blk, width), lambda b, h, c: (p, b * nc + c, h))

    vec = lambda x: x.reshape(1, d)
    kern = functools.partial(_wkv_kernel, chunk=WKV_CHUNK, rows=rows, pairs=pairs, sub=sub)
    return pl.pallas_call(
        kern,
        grid=(nb, npair // pairs, nc),
        in_specs=[proj(0), proj(1), proj(2), proj(3),
                  pl.BlockSpec((blk, width), tok), pl.BlockSpec((blk, width), tok),
                  pl.BlockSpec((1, width), par), pl.BlockSpec((1, width), par),
                  pl.BlockSpec((1, width), par), pl.BlockSpec((1, width), par),
                  pl.BlockSpec((1, width), par),
                  pl.BlockSpec((None, 2 * pairs, HEAD_A, HEAD_A), st)],
        out_specs=[pl.BlockSpec((blk, width), tok),
                   pl.BlockSpec((None, 2 * pairs, HEAD_A, HEAD_A), st)],
        out_shape=[jax.ShapeDtypeStruct((m, d), BF16 if rows % (2 * SUBLANES) == 0 else F32),
                   jax.ShapeDtypeStruct((nb, 2 * npair, HEAD_A, HEAD_A), F32)],
        scratch_shapes=[pltpu.VMEM((pairs, LANES, LANES), F32)],
        compiler_params=_params("arbitrary", "arbitrary", "arbitrary"),
        name="wkv",
    )(rkvg, rkvg, rkvg, rkvg, lw, a, vec(k_k), vec(k_a), vec(r_k), vec(lnx_g), vec(lnx_b), s0)


def _cast_weight_once(w_ref, wb_ref):
    @pl.when(pl.program_id(0) == 0)
    def _():
        wb_ref[...] = w_ref[...].astype(BF16)


def _weight_spec(k, n, col):
    return pl.BlockSpec((k, n), lambda i: (0, col), pipeline_mode=pl.Buffered(1))


def _mm_res_kernel(x_ref, w_ref, r_ref, g_ref, o_ref, wb_ref, *, final_norm):
    _cast_weight_once(w_ref, wb_ref)
    y = r_ref[...] + jnp.dot(x_ref[...].astype(BF16), wb_ref[...], preferred_element_type=F32)
    if final_norm:
        y = _rms(y, g_ref[...])
    o_ref[...] = y


def _mm_res_cols_kernel(x_ref, w_ref, r_ref, g_ref, o_ref, *, final_norm):
    j = pl.program_id(0)
    cols = pl.ds(pl.multiple_of(j * COL_TILE, COL_TILE), COL_TILE)
    o_ref[:, cols] = r_ref[:, cols] + jnp.dot(x_ref[...].astype(BF16), w_ref[...].astype(BF16),
                                              preferred_element_type=F32)
    if final_norm:
        @pl.when(j == pl.num_programs(0) - 1)
        def _():
            o_ref[...] = _rms(o_ref[...], g_ref[...])


def _mm_res(x, w, res, gain=None):
    m, k = x.shape
    n = w.shape[1]
    bm = min(ROW_TILE, m)
    assert m % bm == 0
    final_norm = gain is not None
    g = (gain if final_norm else jnp.ones((n,), F32)).reshape(1, n)
    if m == bm and n % COL_TILE == 0:
        whole = lambda j: (0, 0)
        return pl.pallas_call(
            functools.partial(_mm_res_cols_kernel, final_norm=final_norm),
            grid=(n // COL_TILE,),
            in_specs=[pl.BlockSpec((m, k), whole),
                      pl.BlockSpec((k, COL_TILE), lambda j: (0, j)),
                      pl.BlockSpec((m, n), whole),
                      pl.BlockSpec((1, n), whole)],
            out_specs=pl.BlockSpec((m, n), whole),
            out_shape=jax.ShapeDtypeStruct((m, n), F32),
            compiler_params=_params("arbitrary"),
            name="mm_res_cols",
        )(x, w, res, g)
    return pl.pallas_call(
        functools.partial(_mm_res_kernel, final_norm=final_norm),
        grid=(m // bm,),
        in_specs=[pl.BlockSpec((bm, k), lambda i: (i, 0)),
                  _weight_spec(k, n, 0),
                  pl.BlockSpec((bm, n), lambda i: (i, 0)),
                  pl.BlockSpec((1, n), lambda i: (0, 0))],
        out_specs=pl.BlockSpec((bm, n), lambda i: (i, 0)),
        out_shape=jax.ShapeDtypeStruct((m, n), F32),
        scratch_shapes=[pltpu.VMEM((k, n), BF16)],
        compiler_params=_params("arbitrary"),
        name="mm_res",
    )(x, w, res, g)


def _norm_mm_kernel(x_ref, g_ref, w_ref, *refs, resident_weight, n_out):
    outs = refs[:n_out]
    if resident_weight:
        wb_ref = refs[n_out]
        _cast_weight_once(w_ref, wb_ref)
        w = wb_ref[...]
    else:
        w = w_ref[...].astype(BF16)
    xn = _rms(x_ref[...], g_ref[...]).astype(BF16)
    y = jnp.dot(xn, w, preferred_element_type=F32)
    for o_ref in outs:
        o_ref[...] = y.astype(o_ref.dtype)


def _norm_mm(x, gain, w, col, n, with_bf16=False):
    m, k = x.shape
    bm = min(ROW_TILE, m)
    assert m % bm == 0 and w.shape[1] % n == 0
    dtypes = [F32, BF16] if with_bf16 else [F32]
    out_shape = [jax.ShapeDtypeStruct((m, n), dt) for dt in dtypes]
    if m == bm and n % COL_TILE == 0:
        nj = n // COL_TILE
        outs = pl.pallas_call(
            functools.partial(_norm_mm_kernel, resident_weight=False, n_out=len(dtypes)),
            grid=(nj,),
            in_specs=[pl.BlockSpec((m, k), lambda j: (0, 0)),
                      pl.BlockSpec((1, k), lambda j: (0, 0)),
                      pl.BlockSpec((k, COL_TILE), lambda j: (0, col * nj + j))],
            out_specs=[pl.BlockSpec((m, COL_TILE), lambda j: (0, j)) for _ in dtypes],
            out_shape=out_shape,
            compiler_params=_params("arbitrary"),
            name="norm_mm_cols",
        )(x, gain.reshape(1, k), w)
    else:
        outs = pl.pallas_call(
            functools.partial(_norm_mm_kernel, resident_weight=True, n_out=len(dtypes)),
            grid=(m // bm,),
            in_specs=[pl.BlockSpec((bm, k), lambda i: (i, 0)),
                      pl.BlockSpec((1, k), lambda i: (0, 0)),
                      _weight_spec(k, n, col)],
            out_specs=[pl.BlockSpec((bm, n), lambda i: (i, 0)) for _ in dtypes],
            out_shape=out_shape,
            scratch_shapes=[pltpu.VMEM((k, n), BF16)],
            compiler_params=_params("arbitrary"),
            name="norm_mm",
        )(x, gain.reshape(1, k), w)
    return tuple(outs) if with_bf16 else outs[0]


SB_BLOCK = 256
SB_HEADS = 8
SB_CUM_PIECES = 1
SBS_PAGES = 8


def _sbp_kernel(bias_ref, q_ref, kb_ref, vb_ref, g_ref, o_ref, *, blk, scale, heads):
    hg = pl.program_id(1)
    qi = pl.program_id(2)
    lanes = [slice(h * HEAD_B, (h + 1) * HEAD_B) for h in range(heads)]
    biases = [bias_ref[hg * heads + h] * LOG2E for h in range(heads)]
    qs = [(q_ref[:, ln] * (scale * LOG2E)).astype(BF16) for ln in lanes]
    def suffix_ones(n):
        return jnp.where(lax.broadcasted_iota(jnp.int32, (n, n), 0) > lax.broadcasted_iota(jnp.int32, (n, n), 1),
                         1.0, 0.0).astype(BF16)

    def tile_chain(h, q, keys, carry, mask, suffix, tot, val):
        ln = lanes[h]
        z = lax.dot_general(q, kb_ref[keys, ln], NT, preferred_element_type=F32) + biases[h]
        yield
        sp = _softplus2(z)
        lk = -sp if mask is None else jnp.where(mask, -sp, 0.0)
        log_beta = z - sp
        tot[h] = jnp.sum(lk, axis=1, keepdims=True)
        yield
        rs = _dot(lk, suffix, pa=SB_CUM_PIECES)
        yield
        e = log_beta + rs
        att = jnp.exp2(e if carry is None else e + carry())
        if mask is not None:
            att = jnp.where(mask, att, 0.0)
        yield
        val[h] = jnp.dot(att.astype(BF16), vb_ref[keys, ln], preferred_element_type=F32)

    def run(chains):
        while chains:
            chains = [ch for ch in chains if next(ch, True) is None]

    suffix = suffix_ones(blk)

    causal = lax.broadcasted_iota(jnp.int32, (blk, blk), 1) < lax.broadcasted_iota(jnp.int32, (blk, blk), 0)
    carries, accs = [None] * heads, [None] * heads
    diag_keys = pl.ds(pl.multiple_of(qi * blk, blk), blk)
    run([tile_chain(h, qs[h], diag_keys, None, causal, suffix, carries, accs) for h in range(heads)])

    def block(j, carries, accs):
        keys = pl.ds(pl.multiple_of(j * blk, blk), blk)
        tot, val = [None] * heads, [None] * heads
        run([tile_chain(h, qs[h], keys, functools.partial(carries.__getitem__, h), None, suffix, tot, val)
             for h in range(heads)])
        return [c + t for c, t in zip(carries, tot)], [a + v for a, v in zip(accs, val)]

    def body(it, ca):
        cs, as_ = block(qi - 1 - it, list(ca[:heads]), list(ca[heads:]))
        return tuple(cs) + tuple(as_)

    ca = lax.fori_loop(0, qi, body, tuple(carries) + tuple(accs))
    for h, ln in enumerate(lanes):
        o_ref[:, ln] = (ca[heads + h] * _silu(g_ref[:, ln])).astype(o_ref.dtype)


def _sb_prompt(q, k, v, g, bias, seq):
    m, d = q.shape
    nb = m // seq
    nh = d // HEAD_B
    heads = math.gcd(nh, SB_HEADS)
    width = heads * HEAD_B
    blk = min(SB_BLOCK, seq)
    assert seq % blk == 0
    nq = seq // blk
    qmap = lambda b, h, i: (b * nq + i, h)
    kmap = lambda b, h, i: (b, h)
    kern = functools.partial(_sbp_kernel, blk=blk, scale=1.0 / math.sqrt(HEAD_B), heads=heads)
    return pl.pallas_call(
        kern,
        grid=(nb, nh // heads, nq),
        in_specs=[pl.BlockSpec(memory_space=pltpu.SMEM),
                  pl.BlockSpec((blk, width), qmap),
                  pl.BlockSpec((seq, width), kmap),
                  pl.BlockSpec((seq, width), kmap),
                  pl.BlockSpec((blk, width), qmap)],
        out_specs=pl.BlockSpec((blk, width), qmap),
        out_shape=jax.ShapeDtypeStruct((m, d), BF16),
        compiler_params=_params("arbitrary", "arbitrary", "arbitrary"),
        name="sb_prompt",
    )(bias, q, k, v, g)


def _wide_page(refs, page):
    ht = refs[0].shape[1]
    rows = [r.reshape(page * ht, HEAD_B) for r in refs]
    return jnp.concatenate([r[pl.ds(j, page, stride=ht), :].astype(BF16) for r in rows for j in range(ht)],
                           axis=1)


def _sbs_kernel(pt_ref, qbd_ref, bias_ref, kn_ref, vn_ref, g_ref, *rest, ts, page, nh, nkv):
    del pt_ref
    k_refs, v_refs = rest[:nkv], rest[nkv:2 * nkv]
    o_ref, acc_ref, carry_ref = rest[2 * nkv:]
    ntile = nh // k_refs[0].shape[1]
    p = pl.program_id(1)
    last = pl.num_programs(1) - 1
    cols = nh * ts
    qbd = qbd_ref[...]
    bias = bias_ref[...]
    ri = lax.broadcasted_iota(jnp.int32, (page, page), 0)
    ci = lax.broadcasted_iota(jnp.int32, (page, page), 1)
    suffix = jnp.where(ci > ri, 1.0, 0.0).astype(BF16)

    wide = functools.partial(_wide_page, page=page)

    def process(ks, vs, mask):
        zs = [jnp.dot(k, qbd, preferred_element_type=F32) + bias for k in ks]
        sps = [_softplus2(z) for z in zs]
        lks = [-sp if mask is None else jnp.where(mask, -sp, 0.0) for sp in sps]
        rests = [_dot(suffix, lk, pb=SB_CUM_PIECES) for lk in lks]
        carry = carry_ref[...]
        atts = []
        for z, sp, lk, rs in zip(zs, sps, lks, rests):
            att = jnp.exp2(z - sp + rs + carry)
            if mask is not None:
                att = jnp.where(mask, att, 0.0)
            atts.append(att.astype(BF16))
            carry = carry + jnp.sum(lk, axis=0, keepdims=True)
        att = atts[0] if len(atts) == 1 else jnp.concatenate(atts, axis=0)
        val = vs[0] if len(vs) == 1 else jnp.concatenate(vs, axis=0)
        acc_ref[...] += lax.dot_general(att, val, TN, preferred_element_type=F32)
        carry_ref[...] = carry

    @pl.when(p == 0)
    def _():
        acc_ref[...] = jnp.zeros_like(acc_ref)
        carry_ref[...] = jnp.zeros_like(carry_ref)
        pad = jnp.zeros((page - ts, kn_ref.shape[-1]), BF16)
        kn = jnp.concatenate([kn_ref[...].astype(BF16), pad], axis=0)
        vn = jnp.concatenate([vn_ref[...].astype(BF16), pad], axis=0)
        krow = lax.broadcasted_iota(jnp.int32, (page, cols), 0)
        qcol = lax.broadcasted_iota(jnp.int32, (page, cols), 1) & (ts - 1)
        process([kn], [vn], krow < qcol)

    @pl.when(p > 0)
    def _():
        pages = range(0, nkv, ntile)
        process([wide(k_refs[j:j + ntile]) for j in pages], [wide(v_refs[j:j + ntile]) for j in pages], None)

    @pl.when(p == last)
    def _():
        g = g_ref[...]
        for h in range(nh):
            sl = slice(h * HEAD_B, (h + 1) * HEAD_B)
            o = acc_ref[h * ts:(h + 1) * ts, sl]
            o_ref[:, sl] = (o * _silu(g[:, sl])).astype(o_ref.dtype)


def _sb_sample(q, k_new, v_new, g, cache_k, cache_v, page_table, bias, ts):
    m, d = q.shape
    nb = m // ts
    nh = d // HEAD_B
    n_pool, page = cache_k.shape[:2]
    n_pages = page_table.shape[1]
    cols = nh * ts
    assert ts % SUBLANES == 0 and ts & (ts - 1) == 0
    scale2 = LOG2E / math.sqrt(HEAD_B)
    q4 = (q * scale2).reshape(nb, ts, nh, HEAD_B)
    eye = jnp.eye(nh, dtype=F32)
    qbd = jnp.einsum("bihc,hg->bhcgi", q4, eye).reshape(nb, d, cols).astype(BF16)
    bias_cols = jnp.repeat(bias * LOG2E, ts).reshape(1, cols)
    ht = math.gcd(nh, SUBLANES)
    ntile = nh // ht
    npg = math.gcd(n_pages, SBS_PAGES)
    nkv = npg * ntile
    tok = lambda b, p, pt: (b, 0, 0)

    def page_spec(j, t):
        return pl.BlockSpec((None, page, ht, HEAD_B),
                            lambda b, p, pt: (pt[b, n_pages - 1 - (jnp.maximum(p, 1) - 1) * npg - j], 0, t, 0))

    page_specs = [page_spec(j, t) for j in range(npg) for t in range(ntile)]
    grid_spec = pltpu.PrefetchScalarGridSpec(
        num_scalar_prefetch=1,
        grid=(nb, n_pages // npg + 1),
        in_specs=[pl.BlockSpec((None, d, cols), tok),
                  pl.BlockSpec((1, cols), lambda b, p, pt: (0, 0)),
                  pl.BlockSpec((None, ts, d), tok),
                  pl.BlockSpec((None, ts, d), tok),
                  pl.BlockSpec((None, ts, d), tok)] + page_specs + page_specs,
        out_specs=pl.BlockSpec((None, ts, d), tok),
        scratch_shapes=[pltpu.VMEM((cols, d), F32), pltpu.VMEM((1, cols), F32)],
    )
    kern = functools.partial(_sbs_kernel, ts=ts, page=page, nh=nh, nkv=nkv)
    out = pl.pallas_call(
        kern,
        grid_spec=grid_spec,
        out_shape=jax.ShapeDtypeStruct((nb, ts, d), F32),
        compiler_params=_params("arbitrary", "arbitrary"),
        name="sb_sample",
    )(page_table, qbd, bias_cols, k_new.reshape(nb, ts, d), v_new.reshape(nb, ts, d), g.reshape(nb, ts, d),
      *([cache_k] * nkv), *([cache_v] * nkv))
    return out.reshape(m, d)


def _pad_lora(w_in, w_out):
    r = w_in.shape[1]
    w_in = jnp.pad(w_in, ((0, 0), (0, LORA_PAD - r))).astype(BF16)
    w_out = jnp.pad(w_out, ((0, LORA_PAD - r), (0, 0))).astype(BF16)
    return w_in, w_out


def kernel(x_prompt, x_sample, state_shift, state_wkv, cache_k, cache_v, page_table, a_norm, a_mu, a_w_rkvg, a_w0, a_w1, a_w2, a_a0, a_a1, a_a2, a_k_k, a_k_a, a_r_k, a_lnx_g, a_lnx_b, a_w_o, kv_norm, w_kv, b_norm, b_w_qg, b_logit_bias, b_w_o, final_norm):
    nbp, tp, d = x_prompt.shape
    nbs, ts, _ = x_sample.shape
    n_a = a_norm.shape[0]
    n_b = b_norm.shape[0]
    nh_a = d // HEAD_A
    nh_b = d // HEAD_B

    hp = x_prompt.reshape(nbp * tp, d)
    hs = x_sample.reshape(nbs * ts, d)
    shift_p, wkv_p, shift_s, wkv_s = [], [], [], []

    for i in range(n_a):
        w1, w2 = _pad_lora(a_w1[i], a_w2[i])
        a1, a2 = _pad_lora(a_a1[i], a_a2[i])
        w_rkvg = a_w_rkvg[i]
        w_o = a_w_o[i]
        r_k = a_r_k[i].reshape(d)

        def a_layer(h_in, seq, shift0, s0):
            hn, dx, lw, a = _norm_shift_lora(h_in, shift0, seq, a_norm[i], a_mu[i], w1, w2, a_w0[i],
                                              a1, a2, a_a0[i])
            h_last = hn.reshape(-1, seq, d)[:, -1]
            rkvg = _proj(hn, dx, a_mu[i], w_rkvg)
            og, s_out = _wkv(rkvg, lw, a, s0.astype(F32), seq,
                         a_k_k[i], a_k_a[i], r_k, a_lnx_g[i], a_lnx_b[i])
            h_out = _mm_res(og, w_o, h_in)
            return h_out, h_last, s_out.astype(s0.dtype)

        hp, sh, st = a_layer(hp, tp, jnp.zeros((nbp, d), F32),
                             jnp.zeros((nbp,) + state_wkv.shape[2:], state_wkv.dtype))
        shift_p.append(sh); wkv_p.append(st)
        hs, sh, st = a_layer(hs, ts, state_shift[i], state_wkv[i])
        shift_s.append(sh); wkv_s.append(st)

    k_p, k_p16 = _norm_mm(hp, kv_norm, w_kv, 0, d, with_bf16=True)
    v_p, v_p16 = _norm_mm(hp, kv_norm, w_kv, 1, d, with_bf16=True)
    k_s = _norm_mm(hs, kv_norm, w_kv, 0, d)
    v_s = _norm_mm(hs, kv_norm, w_kv, 1, d)

    for j in range(n_b):
        w_qg = b_w_qg[j]
        w_o = b_w_o[j]
        gain = final_norm if j == n_b - 1 else None
        qp = _norm_mm(hp, b_norm[j], w_qg, 0, d)
        gp = _norm_mm(hp, b_norm[j], w_qg, 1, d)
        op = _sb_prompt(qp, k_p16, v_p16, gp, b_logit_bias[j], tp)
        hp = _mm_res(op, w_o, hp, gain)
        qs = _norm_mm(hs, b_norm[j], w_qg, 0, d)
        gs = _norm_mm(hs, b_norm[j], w_qg, 1, d)
        os_ = _sb_sample(qs, k_s, v_s, gs, cache_k, cache_v, page_table, b_logit_bias[j], ts)
        hs = _mm_res(os_, w_o, hs, gain)

    if n_b == 0:
        raise NotImplementedError("trunk without stick-breaking layers")

    return (hp.reshape(nbp, tp, d), hs.reshape(nbs, ts, d),
            jnp.stack(shift_p), jnp.stack(wkv_p),
            k_p.reshape(nbp, tp, nh_b, HEAD_B), v_p.reshape(nbp, tp, nh_b, HEAD_B),
            jnp.stack(shift_s), jnp.stack(wkv_s),
            k_s.reshape(nbs, ts, nh_b, HEAD_B), v_s.reshape(nbs, ts, nh_b, HEAD_B))
```

```python
import functools
import math

import jax
import jax.numpy as jnp
from jax import lax
from jax.experimental import pallas as pl
from jax.experimental.pallas import tpu as pltpu

F32 = jnp.float32
BF16 = jnp.bfloat16

HEAD_A = 64
HEAD_A_SHIFT = 6
HEAD_B = 128
LANES = 128
SUBLANES = 8
LOG2E = 1.4426950408889634
RMS_EPS = 1e-6
LNX_EPS = 64e-5
LORA_PAD = 128
WKV_CHUNK = 64
WKV_SUBCHUNKS = 4
WKV_PAIRS = 16
WKV_P_MX = 1
WKV_P_INV = 1
WKV_P_APPLY = 1
WKV_P_STATE = 1
VMEM_LIMIT = 56 * 1024 * 1024
ROW_TILE = 512
COL_TILE = 512

NN = (((1,), (0,)), ((), ()))
NT = (((1,), (1,)), ((), ()))
TN = (((0,), (0,)), ((), ()))


def _split(x, n):
    parts = []
    for _ in range(n - 1):
        hi = x.astype(BF16)
        parts.append(hi)
        x = x - hi.astype(F32)
    parts.append(x.astype(BF16))
    return parts


def _dot(a, b, dims=NN, pa=1, pb=1):
    ap = _split(a, pa) if pa > 1 or a.dtype != BF16 else [a]
    bp = _split(b, pb) if pb > 1 or b.dtype != BF16 else [b]
    n = max(pa, pb)
    acc = None
    for j in reversed(range(len(bp))):
        lhs = [ap[i] for i in reversed(range(len(ap))) if i + j < n]
        if dims == TN or len(lhs) == 1:
            terms = [lax.dot_general(x, bp[j], dims, preferred_element_type=F32) for x in lhs]
        else:
            m = a.shape[0]
            t = lax.dot_general(jnp.concatenate(lhs, axis=0), bp[j], dims, preferred_element_type=F32)
            terms = [t[k * m:(k + 1) * m] for k in range(len(lhs))]
        for t in terms:
            acc = t if acc is None else acc + t
    return acc


def _dotp(a, b, dims, level):
    return _dot(a, b, dims, pa=level, pb=level)


def _softplus2(z2):
    return jnp.maximum(z2, 0.0) + jnp.log(1.0 + jnp.exp2(-jnp.abs(z2))) * LOG2E


def _silu(g):
    return g / (1.0 + jnp.exp(-g))


def _rms(x, gain):
    return x * lax.rsqrt(jnp.mean(x * x, axis=-1, keepdims=True) + RMS_EPS) * gain


def _params(*sem):
    return pltpu.CompilerParams(dimension_semantics=sem, vmem_limit_bytes=VMEM_LIMIT)


def _token_shift(x_ref, xh_ref, sh_ref, gain, i, *, bm, seq, per_row_start):
    h = _rms(x_ref[...], gain)
    rolled = pltpu.roll(h, 1, 0)
    row = lax.broadcasted_iota(jnp.int32, (bm, 1), 0)
    if per_row_start:
        h_prev = jnp.where((row & (seq - 1)) == 0, sh_ref[...], rolled)
    else:
        halo = _rms(xh_ref[0], gain)
        first = jnp.where((i * bm) % seq == 0, sh_ref[...], halo[SUBLANES - 1:SUBLANES])
        h_prev = jnp.where(row == 0, first, rolled)
    return h, h_prev - h


def _shift_operands(x, shift, seq, bm, per_row_start, tile_index):
    m, d = x.shape
    nb = m // seq
    if per_row_start:
        sh = jnp.repeat(shift, seq, axis=0)
        sh_spec = pl.BlockSpec((bm, d), lambda *g: (tile_index(*g), 0))
    else:
        sh = shift.reshape(nb, 1, d)
        sh_spec = pl.BlockSpec((None, 1, d), lambda *g: ((tile_index(*g) * bm) // seq, 0, 0))
    xh = x.reshape(m // SUBLANES, SUBLANES, d)
    specs = [pl.BlockSpec((bm, d), lambda *g: (tile_index(*g), 0)),
             pl.BlockSpec((1, SUBLANES, d),
                          lambda *g: (jnp.maximum(tile_index(*g) * (bm // SUBLANES) - 1, 0), 0, 0)),
             sh_spec]
    return [x, xh, sh], specs


def _nsl_kernel(x_ref, xh_ref, sh_ref, gain_ref, mu_ref, w1_ref, w2_ref, w0_ref,
                a1_ref, a2_ref, a0_ref, h_ref, dx_ref, lw_ref, a_ref, *, bm, seq, per_row_start):
    h, dx = _token_shift(x_ref, xh_ref, sh_ref, gain_ref[...], pl.program_id(0),
                         bm=bm, seq=seq, per_row_start=per_row_start)
    h_ref[...] = h
    dx_ref[...] = dx
    x4 = (h + dx * mu_ref[4:5, :]).astype(BF16)
    t = jnp.tanh(jnp.dot(x4, w1_ref[...], preferred_element_type=F32)).astype(BF16)
    wl = jnp.dot(t, w2_ref[...], preferred_element_type=F32) + w0_ref[...]
    lw_ref[...] = -math.exp(-0.5) / (1.0 + jnp.exp(-wl))
    x5 = (h + dx * mu_ref[5:6, :]).astype(BF16)
    u = jnp.dot(x5, a1_ref[...], preferred_element_type=F32).astype(BF16)
    al = jnp.dot(u, a2_ref[...], preferred_element_type=F32) + a0_ref[...]
    a_ref[...] = 1.0 / (1.0 + jnp.exp(-al))


def _shift_tiling(m, seq, row_tile):
    per_row_start = seq < SUBLANES * 2
    bm = m if per_row_start else min(row_tile, seq)
    assert m % bm == 0 and (per_row_start or seq % bm == 0)
    return per_row_start, bm


def _norm_shift_lora(x, shift, seq, gain, mu, w1, w2, w0, a1, a2, a0):
    m, d = x.shape
    per_row_start, bm = _shift_tiling(m, seq, ROW_TILE)
    operands, specs = _shift_operands(x, shift, seq, bm, per_row_start, lambda i: i)
    row = lambda i: (i, 0)
    const = lambda i: (0, 0)
    kern = functools.partial(_nsl_kernel, bm=bm, seq=seq, per_row_start=per_row_start)
    return pl.pallas_call(
        kern,
        grid=(m // bm,),
        in_specs=specs + [
            pl.BlockSpec((1, d), const),
            pl.BlockSpec(mu.shape, const),
            pl.BlockSpec(w1.shape, const),
            pl.BlockSpec(w2.shape, const),
            pl.BlockSpec((1, d), const),
            pl.BlockSpec(a1.shape, const),
            pl.BlockSpec(a2.shape, const),
            pl.BlockSpec((1, d), const),
        ],
        out_specs=[pl.BlockSpec((bm, d), row)] * 4,
        out_shape=[jax.ShapeDtypeStruct((m, d), F32)] * 4,
        compiler_params=_params("arbitrary"),
        name="norm_shift_lora",
    )(*operands, gain.reshape(1, d), mu, w1, w2, w0.reshape(1, d), a1, a2, a0.reshape(1, d))


def _proj_kernel(h_ref, dx_ref, mu_ref, w_ref, o_ref, *wb_ref):
    p = pl.program_id(0)
    if wb_ref:
        @pl.when(pl.program_id(1) == 0)
        def _():
            wb_ref[0][...] = w_ref[...].astype(BF16)
        w = wb_ref[0][...]
    else:
        w = w_ref[...].astype(BF16)
    xs = (h_ref[...] + dx_ref[...] * mu_ref[pl.ds(p, 1), :]).astype(BF16)
    o_ref[...] = jnp.dot(xs, w, preferred_element_type=F32)


def _proj(h, dx, mu, w):
    m, d = h.shape
    np_, k, n = w.shape
    bm = min(ROW_TILE, m)
    assert m % bm == 0
    if m == bm and n % COL_TILE == 0:
        grid = (np_, n // COL_TILE)
        x_spec = pl.BlockSpec((m, d), lambda p, j: (0, 0))
        w_spec = pl.BlockSpec((None, k, COL_TILE), lambda p, j: (p, 0, j))
        o_spec = pl.BlockSpec((None, m, COL_TILE), lambda p, j: (p, 0, j))
        scratch = []
    else:
        grid = (np_, m // bm)
        x_spec = pl.BlockSpec((bm, d), lambda p, i: (i, 0))
        w_spec = pl.BlockSpec((None, k, n), lambda p, i: (p, 0, 0), pipeline_mode=pl.Buffered(1))
        o_spec = pl.BlockSpec((None, bm, n), lambda p, i: (p, i, 0))
        scratch = [pltpu.VMEM((k, n), BF16)]
    return pl.pallas_call(
        _proj_kernel,
        grid=grid,
        in_specs=[x_spec, x_spec, pl.BlockSpec(mu.shape, lambda p, t: (0, 0)), w_spec],
        out_specs=o_spec,
        out_shape=jax.ShapeDtypeStruct((np_, m, n), F32),
        scratch_shapes=scratch,
        compiler_params=_params("arbitrary", "arbitrary"),
        name="proj",
    )(h, dx, mu, w)


def _wkv_kernel(r_ref, k_ref, v_ref, g_ref, w_ref, a_ref, kk_ref, ka_ref, rk_ref, lg_ref, lb_ref,
                s0_ref, o_ref, so_ref, y_ref, *, chunk, rows, pairs, sub):
    c = pl.program_id(2)
    nc = pl.num_programs(2)
    C = chunk
    C2 = 2 * C

    @pl.when(c == 0)
    def _():
        zero = jnp.zeros((HEAD_A, HEAD_A), F32)
        for p in range(pairs):
            top = jnp.concatenate([s0_ref[2 * p], zero], axis=1)
            bot = jnp.concatenate([zero, s0_ref[2 * p + 1]], axis=1)
            y_ref[p] = jnp.concatenate([top, bot], axis=0)

    lane = lax.broadcasted_iota(jnp.int32, (1, LANES), 1)
    head0 = lane < HEAD_A
    ri = lax.broadcasted_iota(jnp.int32, (C2, C2), 0)
    ci = lax.broadcasted_iota(jnp.int32, (C2, C2), 1)
    same_head = (lax.broadcasted_iota(jnp.int32, (LANES, LANES), 0) >> HEAD_A_SHIFT) == (
        lax.broadcasted_iota(jnp.int32, (LANES, LANES), 1) >> HEAD_A_SHIFT)
    si = ri & (C - 1)
    sj = ci & (C - 1)
    strict = sj < si
    incl = sj <= si
    eye = jnp.where(ri == ci, 1.0, 0.0)
    trow = lax.broadcasted_iota(jnp.int32, (C, 1), 0)

    def stack(x):
        return jnp.concatenate([jnp.where(head0, x, 0.0), jnp.where(head0, 0.0, x)], axis=0)

    def pack(x):
        return x[:C] + x[C:]

    def head_sum(x):
        s0 = jnp.sum(jnp.where(head0, x, 0.0), axis=1, keepdims=True)
        s1 = jnp.sum(jnp.where(head0, 0.0, x), axis=1, keepdims=True)
        return jnp.where(head0, s0, s1)

    def pair_chain(p, tok):
        ln = slice(p * LANES, (p + 1) * LANES)

        def load(ref):
            x = ref[tok, ln]
            if rows < C:
                x = jnp.concatenate([x, jnp.zeros((C - rows, LANES), F32)], axis=0)
            return x

        R, K, V, Wl, A = (load(ref) for ref in (r_ref, k_ref, v_ref, w_ref, a_ref))

        kkp = K * kk_ref[:, ln]
        kk = kkp / jnp.maximum(jnp.sqrt(head_sum(kkp * kkp)), 1e-12)
        Km = K * (1.0 + (A - 1.0) * ka_ref[:, ln])

        Lc = Wl
        shift = 1
        while shift < C:
            Lc = Lc + jnp.where(trow >= shift, pltpu.roll(Lc, shift, 0), 0.0)
            shift *= 2
        yield
        Lend = Lc[C - 1:C, :]
        g_inv = jnp.exp(-Lc)
        g_rem = jnp.exp(Lend - Lc)
        kka = kk * A
        Ah = -kk * jnp.exp(Lc - Wl)
        Rh = R * jnp.exp(Lc)
        Bc = kka * g_inv
        Kc = Km * g_inv
        Bt = kka * g_rem
        Kt = Km * g_rem

        sAh = stack(Ah)
        lhs = jnp.concatenate([sAh, stack(Rh)], axis=0)
        rhs = jnp.concatenate([stack(Bc), stack(Kc)], axis=0)
        mx = _dotp(lhs, rhs, NT, WKV_P_MX)
        yield
        Nab =jnp.where(strict, mx[:C2, :C2], 0.0)
        nrest = jnp.concatenate([jnp.where(strict, mx[:C2, C2:], 0.0),
                                 jnp.where(incl, mx[C2:, C2:], 0.0)], axis=0)
        Nrb = jnp.where(incl, mx[C2:, :C2], 0.0)

        Tm = eye + Nab
        Pw = Nab
        nv = _dotp(nrest, stack(V), NN, WKV_P_APPLY)
        yield
        nsq = int(math.log2(C)) - 1
        Pw = _dotp(Pw, Pw, NN, WKV_P_INV)
        yield
        for it in range(nsq):
            if it < nsq - 1:
                both = _dotp(jnp.concatenate([Pw, Tm], axis=0), Pw, NN, WKV_P_INV)
                Pw, Tm = both[:C2], Tm + both[C2:]
            else:
                Tm = Tm + _dotp(Tm, Pw, NN, WKV_P_INV)
            yield

        tx = _dotp(Tm, jnp.concatenate([sAh, nv[:C2]], axis=1), NN, WKV_P_APPLY)
        yield

        Y = y_ref[p]
        sy = _dotp(jnp.concatenate([tx[:, :LANES], Rh], axis=0), Y, NT, WKV_P_STATE)
        yield
        Ub = sy[:C2] + tx[:, LANES:]
        U = pack(Ub)
        O = sy[C2:] + pack(_dotp(Nrb, Ub, NN, WKV_P_STATE) + nv[C2:])
        upd = _dotp(jnp.concatenate([U, V], axis=0), jnp.concatenate([Bt, Kt], axis=0), TN, WKV_P_STATE)
        yield
        y_new = Y * jnp.exp(Lend) + jnp.where(same_head, upd, 0.0)

        mean = head_sum(O) * (1.0 / HEAD_A)
        dlt = O - mean
        var = head_sum(dlt * dlt) * (1.0 / HEAD_A)
        on = dlt * lax.rsqrt(var + LNX_EPS) * lg_ref[:, ln] + lb_ref[:, ln]
        bonus = head_sum(R * Km * rk_ref[:, ln]) * V
        res = (on + bonus) * _silu(load(g_ref))
        yield
        y_ref[p] = y_new
        o_ref[tok, ln] = res[:rows].astype(o_ref.dtype)

    def chunk_step(s, carry):
        tok = pl.ds(pl.multiple_of(s * rows, rows), rows)
        chains = [pair_chain(p, tok) for p in range(pairs)]
        while chains:
            chains = [ch for ch in chains if next(ch, True) is None]
        return carry

    lax.fori_loop(0, sub, chunk_step, 0)

    @pl.when(c == nc - 1)
    def _():
        for p in range(pairs):
            y = y_ref[p]
            so_ref[2 * p] = y[:HEAD_A, :HEAD_A]
            so_ref[2 * p + 1] = y[HEAD_A:, HEAD_A:]


def _wkv(rkvg, lw, a, s0, seq, k_k, k_a, r_k, lnx_g, lnx_b):
    _, m, d = rkvg.shape
    nb = m // seq
    npair = d // LANES
    pairs = math.gcd(npair, WKV_PAIRS)
    width = pairs * LANES
    rows = min(WKV_CHUNK, seq)
    assert seq % rows == 0 and rows % SUBLANES == 0
    sub = math.gcd(seq // rows, WKV_SUBCHUNKS)
    blk = sub * rows
    nc = seq // blk
    tok = lambda b, h, c: (b * nc + c, h)
    par = lambda b, h, c: (0, h)
    st = lambda b, h, c: (b, h, 0, 0)

    def proj(p):
        return pl.BlockSpec((None, blk, width), lambda b, h, c: (p, b * nc + c, h))

    vec = lambda x: x.reshape(1, d)
    kern = functools.partial(_wkv_kernel, chunk=WKV_CHUNK, rows=rows, pairs=pairs, sub=sub)
    return pl.pallas_call(
        kern,
        grid=(nb, npair // pairs, nc),
        in_specs=[proj(0), proj(1), proj(2), proj(3),
                  pl.BlockSpec((blk, width), tok), pl.BlockSpec((blk, width), tok),
                  pl.BlockSpec((1, width), par), pl.BlockSpec((1, width), par),
                  pl.BlockSpec((1, width), par), pl.BlockSpec((1, width), par),
                  pl.BlockSpec((1, width), par),
                  pl.BlockSpec((None, 2 * pairs, HEAD_A, HEAD_A), st)],
        out_specs=[pl.BlockSpec((blk, width), tok),
                   pl.BlockSpec((None, 2 * pairs, HEAD_A, HEAD_A), st)],
        out_shape=[jax.ShapeDtypeStruct((m, d), BF16 if rows % (2 * SUBLANES) == 0 else F32),
                   jax.ShapeDtypeStruct((nb, 2 * npair, HEAD_A, HEAD_A), F32)],
        scratch_shapes=[pltpu.VMEM((pairs, LANES, LANES), F32)],
        compiler_params=_params("arbitrary", "arbitrary", "arbitrary"),
        name="wkv",
    )(rkvg, rkvg, rkvg, rkvg, lw, a, vec(k_k), vec(k_a), vec(r_k), vec(lnx_g), vec(lnx_b), s0)


def _cast_weight_once(w_ref, wb_ref):
    @pl.when(pl.program_id(0) == 0)
    def _():
        wb_ref[...] = w_ref[...].astype(BF16)


def _weight_spec(k, n, col):
    return pl.BlockSpec((k, n), lambda i: (0, col), pipeline_mode=pl.Buffered(1))


def _mm_res_kernel(x_ref, w_ref, r_ref, g_ref, o_ref, wb_ref, *, final_norm):
    _cast_weight_once(w_ref, wb_ref)
    y = r_ref[...] + jnp.dot(x_ref[...].astype(BF16), wb_ref[...], preferred_element_type=F32)
    if final_norm:
        y = _rms(y, g_ref[...])
    o_ref[...] = y


def _mm_res_cols_kernel(x_ref, w_ref, r_ref, g_ref, o_ref, *, final_norm):
    j = pl.program_id(0)
    cols = pl.ds(pl.multiple_of(j * COL_TILE, COL_TILE), COL_TILE)
    o_ref[:, cols] = r_ref[:, cols] + jnp.dot(x_ref[...].astype(BF16), w_ref[...].astype(BF16),
                                              preferred_element_type=F32)
    if final_norm:
        @pl.when(j == pl.num_programs(0) - 1)
        def _():
            o_ref[...] = _rms(o_ref[...], g_ref[...])


def _mm_res(x, w, res, gain=None):
    m, k = x.shape
    n = w.shape[1]
    bm = min(ROW_TILE, m)
    assert m % bm == 0
    final_norm = gain is not None
    g = (gain if final_norm else jnp.ones((n,), F32)).reshape(1, n)
    if m == bm and n % COL_TILE == 0:
        whole = lambda j: (0, 0)
        return pl.pallas_call(
            functools.partial(_mm_res_cols_kernel, final_norm=final_norm),
            grid=(n // COL_TILE,),
            in_specs=[pl.BlockSpec((m, k), whole),
                      pl.BlockSpec((k, COL_TILE), lambda j: (0, j)),
                      pl.BlockSpec((m, n), whole),
                      pl.BlockSpec((1, n), whole)],
            out_specs=pl.BlockSpec((m, n), whole),
            out_shape=jax.ShapeDtypeStruct((m, n), F32),
            compiler_params=_params("arbitrary"),
            name="mm_res_cols",
        )(x, w, res, g)
    return pl.pallas_call(
        functools.partial(_mm_res_kernel, final_norm=final_norm),
        grid=(m // bm,),
        in_specs=[pl.BlockSpec((bm, k), lambda i: (i, 0)),
                  _weight_spec(k, n, 0),
                  pl.BlockSpec((bm, n), lambda i: (i, 0)),
                  pl.BlockSpec((1, n), lambda i: (0, 0))],
        out_specs=pl.BlockSpec((bm, n), lambda i: (i, 0)),
        out_shape=jax.ShapeDtypeStruct((m, n), F32),
        scratch_shapes=[pltpu.VMEM((k, n), BF16)],
        compiler_params=_params("arbitrary"),
        name="mm_res",
    )(x, w, res, g)


def _norm_mm_kernel(x_ref, g_ref, w_ref, *refs, resident_weight, n_out):
    outs = refs[:n_out]
    if resident_weight:
        wb_ref = refs[n_out]
        _cast_weight_once(w_ref, wb_ref)
        w = wb_ref[...]
    else:
        w = w_ref[...].astype(BF16)
    xn = _rms(x_ref[...], g_ref[...]).astype(BF16)
    y = jnp.dot(xn, w, preferred_element_type=F32)
    for o_ref in outs:
        o_ref[...] = y.astype(o_ref.dtype)


def _norm_mm(x, gain, w, col, n, with_bf16=False):
    m, k = x.shape
    bm = min(ROW_TILE, m)
    assert m % bm == 0 and w.shape[1] % n == 0
    dtypes = [F32, BF16] if with_bf16 else [F32]
    out_shape = [jax.ShapeDtypeStruct((m, n), dt) for dt in dtypes]
    if m == bm and n % COL_TILE == 0:
        nj = n // COL_TILE
        outs = pl.pallas_call(
            functools.partial(_norm_mm_kernel, resident_weight=False, n_out=len(dtypes)),
            grid=(nj,),
            in_specs=[pl.BlockSpec((m, k), lambda j: (0, 0)),
                      pl.BlockSpec((1, k), lambda j: (0, 0)),
                      pl.BlockSpec((k, COL_TILE), lambda j: (0, col * nj + j))],
            out_specs=[pl.BlockSpec((m, COL_TILE), lambda j: (0, j)) for _ in dtypes],
            out_shape=out_shape,
            compiler_params=_params("arbitrary"),
            name="norm_mm_cols",
        )(x, gain.reshape(1, k), w)
    else:
        outs = pl.pallas_call(
            functools.partial(_norm_mm_kernel, resident_weight=True, n_out=len(dtypes)),
            grid=(m // bm,),
            in_specs=[pl.BlockSpec((bm, k), lambda i: (i, 0)),
                      pl.BlockSpec((1, k), lambda i: (0, 0)),
                      _weight_spec(k, n, col)],
            out_specs=[pl.BlockSpec((bm, n), lambda i: (i, 0)) for _ in dtypes],
            out_shape=out_shape,
            scratch_shapes=[pltpu.VMEM((k, n), BF16)],
            compiler_params=_params("arbitrary"),
            name="norm_mm",
        )(x, gain.reshape(1, k), w)
    return tuple(outs) if with_bf16 else outs[0]


SB_BLOCK = 256
SB_HEADS = 8
SB_CUM_PIECES = 1
SBS_PAGES = 4
SBS_SLOTS = 16
SBS_AHEAD = 12


def _sbp_kernel(bias_ref, q_ref, kb_ref, vb_ref, g_ref, o_ref, *, blk, scale, heads):
    hg = pl.program_id(1)
    qi = pl.program_id(2)
    lanes = [slice(h * HEAD_B, (h + 1) * HEAD_B) for h in range(heads)]
    biases = [bias_ref[hg * heads + h] * LOG2E for h in range(heads)]
    qs = [(q_ref[:, ln] * (scale * LOG2E)).astype(BF16) for ln in lanes]
    def suffix_ones(n):
        return jnp.where(lax.broadcasted_iota(jnp.int32, (n, n), 0) > lax.broadcasted_iota(jnp.int32, (n, n), 1),
                         1.0, 0.0).astype(BF16)

    def tile_chain(h, q, keys, carry, mask, suffix, tot, val):
        ln = lanes[h]
        z = lax.dot_general(q, kb_ref[keys, ln], NT, preferred_element_type=F32) + biases[h]
        yield
        sp = _softplus2(z)
        lk = -sp if mask is None else jnp.where(mask, -sp, 0.0)
        log_beta = z - sp
        tot[h] = jnp.sum(lk, axis=1, keepdims=True)
        yield
        rs = _dot(lk, suffix, pa=SB_CUM_PIECES)
        yield
        e = log_beta + rs
        att = jnp.exp2(e if carry is None else e + carry())
        if mask is not None:
            att = jnp.where(mask, att, 0.0)
        yield
        val[h] = jnp.dot(att.astype(BF16), vb_ref[keys, ln], preferred_element_type=F32)

    def run(chains):
        while chains:
            chains = [ch for ch in chains if next(ch, True) is None]

    suffix = suffix_ones(blk)

    causal = lax.broadcasted_iota(jnp.int32, (blk, blk), 1) < lax.broadcasted_iota(jnp.int32, (blk, blk), 0)
    carries, accs = [None] * heads, [None] * heads
    diag_keys = pl.ds(pl.multiple_of(qi * blk, blk), blk)
    run([tile_chain(h, qs[h], diag_keys, None, causal, suffix, carries, accs) for h in range(heads)])

    def block(j, carries, accs):
        keys = pl.ds(pl.multiple_of(j * blk, blk), blk)
        tot, val = [None] * heads, [None] * heads
        run([tile_chain(h, qs[h], keys, functools.partial(carries.__getitem__, h), None, suffix, tot, val)
             for h in range(heads)])
        return [c + t for c, t in zip(carries, tot)], [a + v for a, v in zip(accs, val)]

    def body(it, ca):
        cs, as_ = block(qi - 1 - it, list(ca[:heads]), list(ca[heads:]))
        return tuple(cs) + tuple(as_)

    ca = lax.fori_loop(0, qi, body, tuple(carries) + tuple(accs))
    for h, ln in enumerate(lanes):
        o_ref[:, ln] = (ca[heads + h] * _silu(g_ref[:, ln])).astype(o_ref.dtype)


def _sb_prompt(q, k, v, g, bias, seq):
    m, d = q.shape
    nb = m // seq
    nh = d // HEAD_B
    heads = math.gcd(nh, SB_HEADS)
    width = heads * HEAD_B
    blk = min(SB_BLOCK, seq)
    assert seq % blk == 0
    nq = seq // blk
    qmap = lambda b, h, i: (b * nq + i, h)
    kmap = lambda b, h, i: (b, h)
    kern = functools.partial(_sbp_kernel, blk=blk, scale=1.0 / math.sqrt(HEAD_B), heads=heads)
    return pl.pallas_call(
        kern,
        grid=(nb, nh // heads, nq),
        in_specs=[pl.BlockSpec(memory_space=pltpu.SMEM),
                  pl.BlockSpec((blk, width), qmap),
                  pl.BlockSpec((seq, width), kmap),
                  pl.BlockSpec((seq, width), kmap),
                  pl.BlockSpec((blk, width), qmap)],
        out_specs=pl.BlockSpec((blk, width), qmap),
        out_shape=jax.ShapeDtypeStruct((m, d), BF16),
        compiler_params=_params("arbitrary", "arbitrary", "arbitrary"),
        name="sb_prompt",
    )(bias, q, k, v, g)


def _wide_page(buf, first, page, ntile):
    ht = buf.shape[1]
    flat = buf.reshape(buf.shape[0] * ht, HEAD_B)
    return jnp.concatenate([flat[pl.ds((first + t * page) * ht + j, page, stride=ht), :].astype(BF16)
                            for t in range(ntile) for j in range(ht)], axis=1)


def _sbs_kernel(pt_ref, qbd_ref, bias_ref, kn_ref, vn_ref, g_ref, ck_ref, cv_ref, o_ref,
                acc_ref, carry_ref, kbuf, vbuf, ksem, vsem, *, ts, page, nh, n_pages, group):
    b = pl.program_id(0)
    nb = pl.num_programs(0)
    ht = kbuf.shape[1]
    ntile = nh // ht
    cols = nh * ts
    qbd = qbd_ref[...]
    bias = bias_ref[...]
    ri = lax.broadcasted_iota(jnp.int32, (page, page), 0)
    ci = lax.broadcasted_iota(jnp.int32, (page, page), 1)
    suffix = jnp.where(ci > ri, 1.0, 0.0).astype(BF16)

    def first_row(gp):
        return lax.rem(gp, SBS_SLOTS) * (ntile * page)

    def page_copies(gp):
        seq_i = lax.div(gp, n_pages)
        phys = pt_ref[seq_i, n_pages - 1 - lax.rem(gp, n_pages)]
        slot = lax.rem(gp, SBS_SLOTS)
        out = []
        for t in range(ntile):
            rows = pl.ds(first_row(gp) + t * page, page)
            heads = pl.ds(t * ht, ht)
            out.append(pltpu.make_async_copy(ck_ref.at[phys, :, heads, :], kbuf.at[rows], ksem.at[slot]))
            out.append(pltpu.make_async_copy(cv_ref.at[phys, :, heads, :], vbuf.at[rows], vsem.at[slot]))
        return out

    def start_page(gp):
        @pl.when(gp < nb * n_pages)
        def _():
            for cp in page_copies(gp):
                cp.start()

    @pl.when(b == 0)
    def _():
        for gp in range(SBS_AHEAD):
            start_page(jnp.int32(gp))

    def process(ks, vs, mask):
        zs = [jnp.dot(k, qbd, preferred_element_type=F32) + bias for k in ks]
        sps = [_softplus2(z) for z in zs]
        lks = [-sp if mask is None else jnp.where(mask, -sp, 0.0) for sp in sps]
        rests = [_dot(suffix, lk, pb=SB_CUM_PIECES) for lk in lks]
        carry = carry_ref[...]
        atts = []
        for z, sp, lk, rs in zip(zs, sps, lks, rests):
            att = jnp.exp2(z - sp + rs + carry)
            if mask is not None:
                att = jnp.where(mask, att, 0.0)
            atts.append(att.astype(BF16))
            carry = carry + jnp.sum(lk, axis=0, keepdims=True)
        att = atts[0] if len(atts) == 1 else jnp.concatenate(atts, axis=0)
        val = vs[0] if len(vs) == 1 else jnp.concatenate(vs, axis=0)
        acc_ref[...] += lax.dot_general(att, val, TN, preferred_element_type=F32)
        carry_ref[...] = carry

    acc_ref[...] = jnp.zeros_like(acc_ref)
    carry_ref[...] = jnp.zeros_like(carry_ref)
    pad = jnp.zeros((page - ts, kn_ref.shape[-1]), BF16)
    kn = jnp.concatenate([kn_ref[...].astype(BF16), pad], axis=0)
    vn = jnp.concatenate([vn_ref[...].astype(BF16), pad], axis=0)
    krow = lax.broadcasted_iota(jnp.int32, (page, cols), 0)
    qcol = lax.broadcasted_iota(jnp.int32, (page, cols), 1) & (ts - 1)
    process([kn], [vn], krow < qcol)

    def page_group(it, carry):
        gp0 = b * n_pages + it * group
        for j in range(group):
            for cp in page_copies(gp0 + j):
                cp.wait()
        for j in range(group):
            start_page(gp0 + SBS_AHEAD + j)
        process([_wide_page(kbuf, first_row(gp0 + j), page, ntile) for j in range(group)],
                [_wide_page(vbuf, first_row(gp0 + j), page, ntile) for j in range(group)], None)
        return carry

    lax.fori_loop(0, n_pages // group, page_group, 0)

    g = g_ref[...]
    for h in range(nh):
        sl = slice(h * HEAD_B, (h + 1) * HEAD_B)
        o = acc_ref[h * ts:(h + 1) * ts, sl]
        o_ref[:, sl] = (o * _silu(g[:, sl])).astype(o_ref.dtype)


def _sb_sample(q, k_new, v_new, g, cache_k, cache_v, page_table, bias, ts):
    m, d = q.shape
    nb = m // ts
    nh = d // HEAD_B
    n_pool, page = cache_k.shape[:2]
    n_pages = page_table.shape[1]
    cols = nh * ts
    assert ts % SUBLANES == 0 and ts & (ts - 1) == 0
    scale2 = LOG2E / math.sqrt(HEAD_B)
    q4 = (q * scale2).reshape(nb, ts, nh, HEAD_B)
    eye = jnp.eye(nh, dtype=F32)
    qbd = jnp.einsum("bihc,hg->bhcgi", q4, eye).reshape(nb, d, cols).astype(BF16)
    bias_cols = jnp.repeat(bias * LOG2E, ts).reshape(1, cols)
    ht = math.gcd(nh, SUBLANES)
    ntile = nh // ht
    group = math.gcd(n_pages, SBS_PAGES)
    assert SBS_SLOTS % group == 0 and SBS_AHEAD % group == 0 and SBS_AHEAD + group <= SBS_SLOTS
    tok = lambda b, pt: (b, 0, 0)
    ring = pltpu.VMEM((SBS_SLOTS * ntile * page, ht, HEAD_B), F32)
    grid_spec = pltpu.PrefetchScalarGridSpec(
        num_scalar_prefetch=1,
        grid=(nb,),
        in_specs=[pl.BlockSpec((None, d, cols), tok),
                  pl.BlockSpec((1, cols), lambda b, pt: (0, 0)),
                  pl.BlockSpec((None, ts, d), tok),
                  pl.BlockSpec((None, ts, d), tok),
                  pl.BlockSpec((None, ts, d), tok),
                  pl.BlockSpec(memory_space=pl.ANY),
                  pl.BlockSpec(memory_space=pl.ANY)],
        out_specs=pl.BlockSpec((None, ts, d), tok),
        scratch_shapes=[pltpu.VMEM((cols, d), F32), pltpu.VMEM((1, cols), F32), ring, ring,
                        pltpu.SemaphoreType.DMA((SBS_SLOTS,)), pltpu.SemaphoreType.DMA((SBS_SLOTS,))],
    )
    kern = functools.partial(_sbs_kernel, ts=ts, page=page, nh=nh, n_pages=n_pages, group=group)
    out = pl.pallas_call(
        kern,
        grid_spec=grid_spec,
        out_shape=jax.ShapeDtypeStruct((nb, ts, d), F32),
        compiler_params=_params("arbitrary"),
        name="sb_sample",
    )(page_table, qbd, bias_cols, k_new.reshape(nb, ts, d), v_new.reshape(nb, ts, d), g.reshape(nb, ts, d),
      cache_k, cache_v)
    return out.reshape(m, d)


def _pad_lora(w_in, w_out):
    r = w_in.shape[1]
    w_in = jnp.pad(w_in, ((0, 0), (0, LORA_PAD - r))).astype(BF16)
    w_out = jnp.pad(w_out, ((0, LORA_PAD - r), (0, 0))).astype(BF16)
    return w_in, w_out


def kernel(x_prompt, x_sample, state_shift, state_wkv, cache_k, cache_v, page_table, a_norm, a_mu, a_w_rkvg, a_w0, a_w1, a_w2, a_a0, a_a1, a_a2, a_k_k, a_k_a, a_r_k, a_lnx_g, a_lnx_b, a_w_o, kv_norm, w_kv, b_norm, b_w_qg, b_logit_bias, b_w_o, final_norm):
    nbp, tp, d = x_prompt.shape
    nbs, ts, _ = x_sample.shape
    n_a = a_norm.shape[0]
    n_b = b_norm.shape[0]
    nh_a = d // HEAD_A
    nh_b = d // HEAD_B

    hp = x_prompt.reshape(nbp * tp, d)
    hs = x_sample.reshape(nbs * ts, d)
    shift_p, wkv_p, shift_s, wkv_s = [], [], [], []

    for i in range(n_a):
        w1, w2 = _pad_lora(a_w1[i], a_w2[i])
        a1, a2 = _pad_lora(a_a1[i], a_a2[i])
        w_rkvg = a_w_rkvg[i]
        w_o = a_w_o[i]
        r_k = a_r_k[i].reshape(d)

        def a_layer(h_in, seq, shift0, s0):
            hn, dx, lw, a = _norm_shift_lora(h_in, shift0, seq, a_norm[i], a_mu[i], w1, w2, a_w0[i],
                                              a1, a2, a_a0[i])
            h_last = hn.reshape(-1, seq, d)[:, -1]
            rkvg = _proj(hn, dx, a_mu[i], w_rkvg)
            og, s_out = _wkv(rkvg, lw, a, s0.astype(F32), seq,
                         a_k_k[i], a_k_a[i], r_k, a_lnx_g[i], a_lnx_b[i])
            h_out = _mm_res(og, w_o, h_in)
            return h_out, h_last, s_out.astype(s0.dtype)

        hp, sh, st = a_layer(hp, tp, jnp.zeros((nbp, d), F32),
                             jnp.zeros((nbp,) + state_wkv.shape[2:], state_wkv.dtype))
        shift_p.append(sh); wkv_p.append(st)
        hs, sh, st = a_layer(hs, ts, state_shift[i], state_wkv[i])
        shift_s.append(sh); wkv_s.append(st)

    k_p, k_p16 = _norm_mm(hp, kv_norm, w_kv, 0, d, with_bf16=True)
    v_p, v_p16 = _norm_mm(hp, kv_norm, w_kv, 1, d, with_bf16=True)
    k_s = _norm_mm(hs, kv_norm, w_kv, 0, d)
    v_s = _norm_mm(hs, kv_norm, w_kv, 1, d)

    for j in range(n_b):
        w_qg = b_w_qg[j]
        w_o = b_w_o[j]
        gain = final_norm if j == n_b - 1 else None
        qp = _norm_mm(hp, b_norm[j], w_qg, 0, d)
        gp = _norm_mm(hp, b_norm[j], w_qg, 1, d)
        op = _sb_prompt(qp, k_p16, v_p16, gp, b_logit_bias[j], tp)
        hp = _mm_res(op, w_o, hp, gain)
        qs = _norm_mm(hs, b_norm[j], w_qg, 0, d)
        gs = _norm_mm(hs, b_norm[j], w_qg, 1, d)
        os_ = _sb_sample(qs, k_s, v_s, gs, cache_k, cache_v, page_table, b_logit_bias[j], ts)
        hs = _mm_res(os_, w_o, hs, gain)

    if n_b == 0:
        raise NotImplementedError("trunk without stick-breaking layers")

    return (hp.reshape(nbp, tp, d), hs.reshape(nbs, ts, d),
            jnp.stack(shift_p), jnp.stack(wkv_p),
            k_p.reshape(nbp, tp, nh_b, HEAD_B), v_p.reshape(nbp, tp, nh_b, HEAD_B),
            jnp.stack(shift_s), jnp.stack(wkv_s),
            k_s.reshape(nbs, ts, nh_b, HEAD_B), v_s.reshape(nbs, ts, nh_b, HEAD_B))
```

```python
import functools
import math

import jax
import jax.numpy as jnp
from jax import lax
from jax.experimental import pallas as pl
from jax.experimental.pallas import tpu as pltpu

F32 = jnp.float32
BF16 = jnp.bfloat16

HEAD_A = 64
HEAD_A_SHIFT = 6
HEAD_B = 128
LANES = 128
SUBLANES = 8
LOG2E = 1.4426950408889634
RMS_EPS = 1e-6
LNX_EPS = 64e-5
LORA_PAD = 128
WKV_CHUNK = 64
WKV_SUBCHUNKS = 4
WKV_PAIRS = 16
WKV_P_MX = 1
WKV_P_INV = 1
WKV_P_APPLY = 1
WKV_P_STATE = 1
VMEM_LIMIT = 56 * 1024 * 1024
ROW_TILE = 512
COL_TILE = 512

NN = (((1,), (0,)), ((), ()))
NT = (((1,), (1,)), ((), ()))
TN = (((0,), (0,)), ((), ()))


def _split(x, n):
    parts = []
    for _ in range(n - 1):
        hi = x.astype(BF16)
        parts.append(hi)
        x = x - hi.astype(F32)
    parts.append(x.astype(BF16))
    return parts


def _dot(a, b, dims=NN, pa=1, pb=1):
    ap = _split(a, pa) if pa > 1 or a.dtype != BF16 else [a]
    bp = _split(b, pb) if pb > 1 or b.dtype != BF16 else [b]
    n = max(pa, pb)
    acc = None
    for j in reversed(range(len(bp))):
        lhs = [ap[i] for i in reversed(range(len(ap))) if i + j < n]
        if dims == TN or len(lhs) == 1:
            terms = [lax.dot_general(x, bp[j], dims, preferred_element_type=F32) for x in lhs]
        else:
            m = a.shape[0]
            t = lax.dot_general(jnp.concatenate(lhs, axis=0), bp[j], dims, preferred_element_type=F32)
            terms = [t[k * m:(k + 1) * m] for k in range(len(lhs))]
        for t in terms:
            acc = t if acc is None else acc + t
    return acc


def _dotp(a, b, dims, level):
    return _dot(a, b, dims, pa=level, pb=level)


def _softplus2(z2):
    return jnp.maximum(z2, 0.0) + jnp.log(1.0 + jnp.exp2(-jnp.abs(z2))) * LOG2E


def _silu(g):
    return g / (1.0 + jnp.exp(-g))


def _rms(x, gain):
    return x * lax.rsqrt(jnp.mean(x * x, axis=-1, keepdims=True) + RMS_EPS) * gain


def _params(*sem):
    return pltpu.CompilerParams(dimension_semantics=sem, vmem_limit_bytes=VMEM_LIMIT)


def _token_shift(x_ref, xh_ref, sh_ref, gain, i, *, bm, seq, per_row_start):
    h = _rms(x_ref[...], gain)
    rolled = pltpu.roll(h, 1, 0)
    row = lax.broadcasted_iota(jnp.int32, (bm, 1), 0)
    if per_row_start:
        h_prev = jnp.where((row & (seq - 1)) == 0, sh_ref[...], rolled)
    else:
        halo = _rms(xh_ref[0], gain)
        first = jnp.where((i * bm) % seq == 0, sh_ref[...], halo[SUBLANES - 1:SUBLANES])
        h_prev = jnp.where(row == 0, first, rolled)
    return h, h_prev - h


def _shift_operands(x, shift, seq, bm, per_row_start, tile_index):
    m, d = x.shape
    nb = m // seq
    if per_row_start:
        sh = jnp.repeat(shift, seq, axis=0)
        sh_spec = pl.BlockSpec((bm, d), lambda *g: (tile_index(*g), 0))
    else:
        sh = shift.reshape(nb, 1, d)
        sh_spec = pl.BlockSpec((None, 1, d), lambda *g: ((tile_index(*g) * bm) // seq, 0, 0))
    xh = x.reshape(m // SUBLANES, SUBLANES, d)
    specs = [pl.BlockSpec((bm, d), lambda *g: (tile_index(*g), 0)),
             pl.BlockSpec((1, SUBLANES, d),
                          lambda *g: (jnp.maximum(tile_index(*g) * (bm // SUBLANES) - 1, 0), 0, 0)),
             sh_spec]
    return [x, xh, sh], specs


def _nsl_kernel(x_ref, xh_ref, sh_ref, gain_ref, mu_ref, w1_ref, w2_ref, w0_ref,
                a1_ref, a2_ref, a0_ref, h_ref, dx_ref, lw_ref, a_ref, *, bm, seq, per_row_start):
    h, dx = _token_shift(x_ref, xh_ref, sh_ref, gain_ref[...], pl.program_id(0),
                         bm=bm, seq=seq, per_row_start=per_row_start)
    h_ref[...] = h
    dx_ref[...] = dx
    x4 = (h + dx * mu_ref[4:5, :]).astype(BF16)
    t = jnp.tanh(jnp.dot(x4, w1_ref[...], preferred_element_type=F32)).astype(BF16)
    wl = jnp.dot(t, w2_ref[...], preferred_element_type=F32) + w0_ref[...]
    lw_ref[...] = -math.exp(-0.5) / (1.0 + jnp.exp(-wl))
    x5 = (h + dx * mu_ref[5:6, :]).astype(BF16)
    u = jnp.dot(x5, a1_ref[...], preferred_element_type=F32).astype(BF16)
    al = jnp.dot(u, a2_ref[...], preferred_element_type=F32) + a0_ref[...]
    a_ref[...] = 1.0 / (1.0 + jnp.exp(-al))


def _shift_tiling(m, seq, row_tile):
    per_row_start = seq < SUBLANES * 2
    bm = m if per_row_start else min(row_tile, seq)
    assert m % bm == 0 and (per_row_start or seq % bm == 0)
    return per_row_start, bm


def _norm_shift_lora(x, shift, seq, gain, mu, w1, w2, w0, a1, a2, a0):
    m, d = x.shape
    per_row_start, bm = _shift_tiling(m, seq, ROW_TILE)
    operands, specs = _shift_operands(x, shift, seq, bm, per_row_start, lambda i: i)
    row = lambda i: (i, 0)
    const = lambda i: (0, 0)
    kern = functools.partial(_nsl_kernel, bm=bm, seq=seq, per_row_start=per_row_start)
    return pl.pallas_call(
        kern,
        grid=(m // bm,),
        in_specs=specs + [
            pl.BlockSpec((1, d), const),
            pl.BlockSpec(mu.shape, const),
            pl.BlockSpec(w1.shape, const),
            pl.BlockSpec(w2.shape, const),
            pl.BlockSpec((1, d), const),
            pl.BlockSpec(a1.shape, const),
            pl.BlockSpec(a2.shape, const),
            pl.BlockSpec((1, d), const),
        ],
        out_specs=[pl.BlockSpec((bm, d), row)] * 4,
        out_shape=[jax.ShapeDtypeStruct((m, d), F32)] * 4,
        compiler_params=_params("arbitrary"),
        name="norm_shift_lora",
    )(*operands, gain.reshape(1, d), mu, w1, w2, w0.reshape(1, d), a1, a2, a0.reshape(1, d))


def _proj_kernel(h_ref, dx_ref, mu_ref, w_ref, o_ref, *wb_ref):
    p = pl.program_id(0)
    if wb_ref:
        @pl.when(pl.program_id(1) == 0)
        def _():
            wb_ref[0][...] = w_ref[...].astype(BF16)
        w = wb_ref[0][...]
    else:
        w = w_ref[...].astype(BF16)
    xs = (h_ref[...] + dx_ref[...] * mu_ref[pl.ds(p, 1), :]).astype(BF16)
    o_ref[...] = jnp.dot(xs, w, preferred_element_type=F32)


def _proj(h, dx, mu, w):
    m, d = h.shape
    np_, k, n = w.shape
    bm = min(ROW_TILE, m)
    assert m % bm == 0
    if m == bm and n % COL_TILE == 0:
        grid = (np_, n // COL_TILE)
        x_spec = pl.BlockSpec((m, d), lambda p, j: (0, 0))
        w_spec = pl.BlockSpec((None, k, COL_TILE), lambda p, j: (p, 0, j))
        o_spec = pl.BlockSpec((None, m, COL_TILE), lambda p, j: (p, 0, j))
        scratch = []
    else:
        grid = (np_, m // bm)
        x_spec = pl.BlockSpec((bm, d), lambda p, i: (i, 0))
        w_spec = pl.BlockSpec((None, k, n), lambda p, i: (p, 0, 0), pipeline_mode=pl.Buffered(1))
        o_spec = pl.BlockSpec((None, bm, n), lambda p, i: (p, i, 0))
        scratch = [pltpu.VMEM((k, n), BF16)]
    return pl.pallas_call(
        _proj_kernel,
        grid=grid,
        in_specs=[x_spec, x_spec, pl.BlockSpec(mu.shape, lambda p, t: (0, 0)), w_spec],
        out_specs=o_spec,
        out_shape=jax.ShapeDtypeStruct((np_, m, n), F32),
        scratch_shapes=scratch,
        compiler_params=_params("arbitrary", "arbitrary"),
        name="proj",
    )(h, dx, mu, w)


def _wkv_kernel(r_ref, k_ref, v_ref, g_ref, w_ref, a_ref, kk_ref, ka_ref, rk_ref, lg_ref, lb_ref,
                s0_ref, o_ref, so_ref, y_ref, *, chunk, rows, pairs, sub):
    c = pl.program_id(2)
    nc = pl.num_programs(2)
    C = chunk
    C2 = 2 * C

    @pl.when(c == 0)
    def _():
        zero = jnp.zeros((HEAD_A, HEAD_A), F32)
        for p in range(pairs):
            top = jnp.concatenate([s0_ref[2 * p], zero], axis=1)
            bot = jnp.concatenate([zero, s0_ref[2 * p + 1]], axis=1)
            y_ref[p] = jnp.concatenate([top, bot], axis=0)

    lane = lax.broadcasted_iota(jnp.int32, (1, LANES), 1)
    head0 = lane < HEAD_A
    ri = lax.broadcasted_iota(jnp.int32, (C2, C2), 0)
    ci = lax.broadcasted_iota(jnp.int32, (C2, C2), 1)
    same_head = (lax.broadcasted_iota(jnp.int32, (LANES, LANES), 0) >> HEAD_A_SHIFT) == (
        lax.broadcasted_iota(jnp.int32, (LANES, LANES), 1) >> HEAD_A_SHIFT)
    si = ri & (C - 1)
    sj = ci & (C - 1)
    strict = sj < si
    incl = sj <= si
    eye = jnp.where(ri == ci, 1.0, 0.0)
    trow = lax.broadcasted_iota(jnp.int32, (C, 1), 0)

    def stack(x):
        return jnp.concatenate([jnp.where(head0, x, 0.0), jnp.where(head0, 0.0, x)], axis=0)

    def pack(x):
        return x[:C] + x[C:]

    def head_sum(x):
        s0 = jnp.sum(jnp.where(head0, x, 0.0), axis=1, keepdims=True)
        s1 = jnp.sum(jnp.where(head0, 0.0, x), axis=1, keepdims=True)
        return jnp.where(head0, s0, s1)

    def pair_chain(p, tok):
        ln = slice(p * LANES, (p + 1) * LANES)

        def load(ref):
            x = ref[tok, ln]
            if rows < C:
                x = jnp.concatenate([x, jnp.zeros((C - rows, LANES), F32)], axis=0)
            return x

        R, K, V, Wl, A = (load(ref) for ref in (r_ref, k_ref, v_ref, w_ref, a_ref))

        kkp = K * kk_ref[:, ln]
        kk = kkp / jnp.maximum(jnp.sqrt(head_sum(kkp * kkp)), 1e-12)
        Km = K * (1.0 + (A - 1.0) * ka_ref[:, ln])

        Lc = Wl
        shift = 1
        while shift < C:
            Lc = Lc + jnp.where(trow >= shift, pltpu.roll(Lc, shift, 0), 0.0)
            shift *= 2
        yield
        Lend = Lc[C - 1:C, :]
        g_inv = jnp.exp(-Lc)
        g_rem = jnp.exp(Lend - Lc)
        kka = kk * A
        Ah = -kk * jnp.exp(Lc - Wl)
        Rh = R * jnp.exp(Lc)
        Bc = kka * g_inv
        Kc = Km * g_inv
        Bt = kka * g_rem
        Kt = Km * g_rem

        sAh = stack(Ah)
        lhs = jnp.concatenate([sAh, stack(Rh)], axis=0)
        rhs = jnp.concatenate([stack(Bc), stack(Kc)], axis=0)
        mx = _dotp(lhs, rhs, NT, WKV_P_MX)
        yield
        Nab =jnp.where(strict, mx[:C2, :C2], 0.0)
        nrest = jnp.concatenate([jnp.where(strict, mx[:C2, C2:], 0.0),
                                 jnp.where(incl, mx[C2:, C2:], 0.0)], axis=0)
        Nrb = jnp.where(incl, mx[C2:, :C2], 0.0)

        Tm = eye + Nab
        Pw = Nab
        nv = _dotp(nrest, stack(V), NN, WKV_P_APPLY)
        yield
        nsq = int(math.log2(C)) - 1
        Pw = _dotp(Pw, Pw, NN, WKV_P_INV)
        yield
        for it in range(nsq):
            if it < nsq - 1:
                both = _dotp(jnp.concatenate([Pw, Tm], axis=0), Pw, NN, WKV_P_INV)
                Pw, Tm = both[:C2], Tm + both[C2:]
            else:
                Tm = Tm + _dotp(Tm, Pw, NN, WKV_P_INV)
            yield

        tx = _dotp(Tm, jnp.concatenate([sAh, nv[:C2]], axis=1), NN, WKV_P_APPLY)
        yield

        Y = y_ref[p]
        sy = _dotp(jnp.concatenate([tx[:, :LANES], Rh], axis=0), Y, NT, WKV_P_STATE)
        yield
        Ub = sy[:C2] + tx[:, LANES:]
        U = pack(Ub)
        O = sy[C2:] + pack(_dotp(Nrb, Ub, NN, WKV_P_STATE) + nv[C2:])
        upd = _dotp(jnp.concatenate([U, V], axis=0), jnp.concatenate([Bt, Kt], axis=0), TN, WKV_P_STATE)
        yield
        y_new = Y * jnp.exp(Lend) + jnp.where(same_head, upd, 0.0)

        mean = head_sum(O) * (1.0 / HEAD_A)
        dlt = O - mean
        var = head_sum(dlt * dlt) * (1.0 / HEAD_A)
        on = dlt * lax.rsqrt(var + LNX_EPS) * lg_ref[:, ln] + lb_ref[:, ln]
        bonus = head_sum(R * Km * rk_ref[:, ln]) * V
        res = (on + bonus) * _silu(load(g_ref))
        yield
        y_ref[p] = y_new
        o_ref[tok, ln] = res[:rows].astype(o_ref.dtype)

    def chunk_step(s, carry):
        tok = pl.ds(pl.multiple_of(s * rows, rows), rows)
        chains = [pair_chain(p, tok) for p in range(pairs)]
        while chains:
            chains = [ch for ch in chains if next(ch, True) is None]
        return carry

    lax.fori_loop(0, sub, chunk_step, 0)

    @pl.when(c == nc - 1)
    def _():
        for p in range(pairs):
            y = y_ref[p]
            so_ref[2 * p] = y[:HEAD_A, :HEAD_A]
            so_ref[2 * p + 1] = y[HEAD_A:, HEAD_A:]


def _wkv(rkvg, lw, a, s0, seq, k_k, k_a, r_k, lnx_g, lnx_b):
    _, m, d = rkvg.shape
    nb = m // seq
    npair = d // LANES
    pairs = math.gcd(npair, WKV_PAIRS)
    width = pairs * LANES
    rows = min(WKV_CHUNK, seq)
    assert seq % rows == 0 and rows % SUBLANES == 0
    sub = math.gcd(seq // rows, WKV_SUBCHUNKS)
    blk = sub * rows
    nc = seq // blk
    tok = lambda b, h, c: (b * nc + c, h)
    par = lambda b, h, c: (0, h)
    st = lambda b, h, c: (b, h, 0, 0)

    def proj(p):
        return pl.BlockSpec((None, blk, width), lambda b, h, c: (p, b * nc + c, h))

    vec = lambda x: x.reshape(1, d)
    chunk = min(WKV_CHUNK, max(SUBLANES, rows))
    kern = functools.partial(_wkv_kernel, chunk=chunk, rows=rows, pairs=pairs, sub=sub)
    return pl.pallas_call(
        kern,
        grid=(nb, npair // pairs, nc),
        in_specs=[proj(0), proj(1), proj(2), proj(3),
                  pl.BlockSpec((blk, width), tok), pl.BlockSpec((blk, width), tok),
                  pl.BlockSpec((1, width), par), pl.BlockSpec((1, width), par),
                  pl.BlockSpec((1, width), par), pl.BlockSpec((1, width), par),
                  pl.BlockSpec((1, width), par),
                  pl.BlockSpec((None, 2 * pairs, HEAD_A, HEAD_A), st)],
        out_specs=[pl.BlockSpec((blk, width), tok),
                   pl.BlockSpec((None, 2 * pairs, HEAD_A, HEAD_A), st)],
        out_shape=[jax.ShapeDtypeStruct((m, d), BF16 if rows % (2 * SUBLANES) == 0 else F32),
                   jax.ShapeDtypeStruct((nb, 2 * npair, HEAD_A, HEAD_A), F32)],
        scratch_shapes=[pltpu.VMEM((pairs, LANES, LANES), F32)],
        compiler_params=_params("arbitrary", "arbitrary", "arbitrary"),
        name="wkv",
    )(rkvg, rkvg, rkvg, rkvg, lw, a, vec(k_k), vec(k_a), vec(r_k), vec(lnx_g), vec(lnx_b), s0)


def _cast_weight_once(w_ref, wb_ref):
    @pl.when(pl.program_id(0) == 0)
    def _():
        wb_ref[...] = w_ref[...].astype(BF16)


def _weight_spec(k, n, col):
    return pl.BlockSpec((k, n), lambda i: (0, col), pipeline_mode=pl.Buffered(1))


def _mm_res_kernel(x_ref, w_ref, r_ref, g_ref, o_ref, wb_ref, *, final_norm):
    _cast_weight_once(w_ref, wb_ref)
    y = r_ref[...] + jnp.dot(x_ref[...].astype(BF16), wb_ref[...], preferred_element_type=F32)
    if final_norm:
        y = _rms(y, g_ref[...])
    o_ref[...] = y


def _mm_res_cols_kernel(x_ref, w_ref, r_ref, g_ref, o_ref, *, final_norm):
    j = pl.program_id(0)
    cols = pl.ds(pl.multiple_of(j * COL_TILE, COL_TILE), COL_TILE)
    o_ref[:, cols] = r_ref[:, cols] + jnp.dot(x_ref[...].astype(BF16), w_ref[...].astype(BF16),
                                              preferred_element_type=F32)
    if final_norm:
        @pl.when(j == pl.num_programs(0) - 1)
        def _():
            o_ref[...] = _rms(o_ref[...], g_ref[...])


def _mm_res(x, w, res, gain=None):
    m, k = x.shape
    n = w.shape[1]
    bm = min(ROW_TILE, m)
    assert m % bm == 0
    final_norm = gain is not None
    g = (gain if final_norm else jnp.ones((n,), F32)).reshape(1, n)
    if m == bm and n % COL_TILE == 0:
        whole = lambda j: (0, 0)
        return pl.pallas_call(
            functools.partial(_mm_res_cols_kernel, final_norm=final_norm),
            grid=(n // COL_TILE,),
            in_specs=[pl.BlockSpec((m, k), whole),
                      pl.BlockSpec((k, COL_TILE), lambda j: (0, j)),
                      pl.BlockSpec((m, n), whole),
                      pl.BlockSpec((1, n), whole)],
            out_specs=pl.BlockSpec((m, n), whole),
            out_shape=jax.ShapeDtypeStruct((m, n), F32),
            compiler_params=_params("arbitrary"),
            name="mm_res_cols",
        )(x, w, res, g)
    return pl.pallas_call(
        functools.partial(_mm_res_kernel, final_norm=final_norm),
        grid=(m // bm,),
        in_specs=[pl.BlockSpec((bm, k), lambda i: (i, 0)),
                  _weight_spec(k, n, 0),
                  pl.BlockSpec((bm, n), lambda i: (i, 0)),
                  pl.BlockSpec((1, n), lambda i: (0, 0))],
        out_specs=pl.BlockSpec((bm, n), lambda i: (i, 0)),
        out_shape=jax.ShapeDtypeStruct((m, n), F32),
        scratch_shapes=[pltpu.VMEM((k, n), BF16)],
        compiler_params=_params("arbitrary"),
        name="mm_res",
    )(x, w, res, g)


def _norm_mm_kernel(x_ref, g_ref, w_ref, *refs, resident_weight, n_out):
    outs = refs[:n_out]
    if resident_weight:
        wb_ref = refs[n_out]
        _cast_weight_once(w_ref, wb_ref)
        w = wb_ref[...]
    else:
        w = w_ref[...].astype(BF16)
    xn = _rms(x_ref[...], g_ref[...]).astype(BF16)
    y = jnp.dot(xn, w, preferred_element_type=F32)
    for o_ref in outs:
        o_ref[...] = y.astype(o_ref.dtype)


def _norm_mm(x, gain, w, col, n, with_bf16=False):
    m, k = x.shape
    bm = min(ROW_TILE, m)
    assert m % bm == 0 and w.shape[1] % n == 0
    dtypes = [F32, BF16] if with_bf16 else [F32]
    out_shape = [jax.ShapeDtypeStruct((m, n), dt) for dt in dtypes]
    if m == bm and n % COL_TILE == 0:
        nj = n // COL_TILE
        outs = pl.pallas_call(
            functools.partial(_norm_mm_kernel, resident_weight=False, n_out=len(dtypes)),
            grid=(nj,),
            in_specs=[pl.BlockSpec((m, k), lambda j: (0, 0)),
                      pl.BlockSpec((1, k), lambda j: (0, 0)),
                      pl.BlockSpec((k, COL_TILE), lambda j: (0, col * nj + j))],
            out_specs=[pl.BlockSpec((m, COL_TILE), lambda j: (0, j)) for _ in dtypes],
            out_shape=out_shape,
            compiler_params=_params("arbitrary"),
            name="norm_mm_cols",
        )(x, gain.reshape(1, k), w)
    else:
        outs = pl.pallas_call(
            functools.partial(_norm_mm_kernel, resident_weight=True, n_out=len(dtypes)),
            grid=(m // bm,),
            in_specs=[pl.BlockSpec((bm, k), lambda i: (i, 0)),
                      pl.BlockSpec((1, k), lambda i: (0, 0)),
                      _weight_spec(k, n, col)],
            out_specs=[pl.BlockSpec((bm, n), lambda i: (i, 0)) for _ in dtypes],
            out_shape=out_shape,
            scratch_shapes=[pltpu.VMEM((k, n), BF16)],
            compiler_params=_params("arbitrary"),
            name="norm_mm",
        )(x, gain.reshape(1, k), w)
    return tuple(outs) if with_bf16 else outs[0]


SB_BLOCK = 256
SB_HEADS = 8
SB_CUM_PIECES = 1
SBS_PAGES = 4
SBS_SLOTS = 16
SBS_AHEAD = 12


def _sbp_kernel(bias_ref, q_ref, kb_ref, vb_ref, g_ref, o_ref, *, blk, scale, heads):
    hg = pl.program_id(1)
    qi = pl.program_id(2)
    lanes = [slice(h * HEAD_B, (h + 1) * HEAD_B) for h in range(heads)]
    biases = [bias_ref[hg * heads + h] * LOG2E for h in range(heads)]
    qs = [(q_ref[:, ln] * (scale * LOG2E)).astype(BF16) for ln in lanes]
    def suffix_ones(n):
        return jnp.where(lax.broadcasted_iota(jnp.int32, (n, n), 0) > lax.broadcasted_iota(jnp.int32, (n, n), 1),
                         1.0, 0.0).astype(BF16)

    def tile_chain(h, q, keys, carry, mask, suffix, tot, val):
        ln = lanes[h]
        z = lax.dot_general(q, kb_ref[keys, ln], NT, preferred_element_type=F32) + biases[h]
        yield
        sp = _softplus2(z)
        lk = -sp if mask is None else jnp.where(mask, -sp, 0.0)
        log_beta = z - sp
        tot[h] = jnp.sum(lk, axis=1, keepdims=True)
        yield
        rs = _dot(lk, suffix, pa=SB_CUM_PIECES)
        yield
        e = log_beta + rs
        att = jnp.exp2(e if carry is None else e + carry())
        if mask is not None:
            att = jnp.where(mask, att, 0.0)
        yield
        val[h] = jnp.dot(att.astype(BF16), vb_ref[keys, ln], preferred_element_type=F32)

    def run(chains):
        while chains:
            chains = [ch for ch in chains if next(ch, True) is None]

    suffix = suffix_ones(blk)

    causal = lax.broadcasted_iota(jnp.int32, (blk, blk), 1) < lax.broadcasted_iota(jnp.int32, (blk, blk), 0)
    carries, accs = [None] * heads, [None] * heads
    diag_keys = pl.ds(pl.multiple_of(qi * blk, blk), blk)
    run([tile_chain(h, qs[h], diag_keys, None, causal, suffix, carries, accs) for h in range(heads)])

    def block(j, carries, accs):
        keys = pl.ds(pl.multiple_of(j * blk, blk), blk)
        tot, val = [None] * heads, [None] * heads
        run([tile_chain(h, qs[h], keys, functools.partial(carries.__getitem__, h), None, suffix, tot, val)
             for h in range(heads)])
        return [c + t for c, t in zip(carries, tot)], [a + v for a, v in zip(accs, val)]

    def body(it, ca):
        cs, as_ = block(qi - 1 - it, list(ca[:heads]), list(ca[heads:]))
        return tuple(cs) + tuple(as_)

    ca = lax.fori_loop(0, qi, body, tuple(carries) + tuple(accs))
    for h, ln in enumerate(lanes):
        o_ref[:, ln] = (ca[heads + h] * _silu(g_ref[:, ln])).astype(o_ref.dtype)


def _sb_prompt(q, k, v, g, bias, seq):
    m, d = q.shape
    nb = m // seq
    nh = d // HEAD_B
    heads = math.gcd(nh, SB_HEADS)
    width = heads * HEAD_B
    blk = min(SB_BLOCK, seq)
    assert seq % blk == 0
    nq = seq // blk
    qmap = lambda b, h, i: (b * nq + i, h)
    kmap = lambda b, h, i: (b, h)
    kern = functools.partial(_sbp_kernel, blk=blk, scale=1.0 / math.sqrt(HEAD_B), heads=heads)
    return pl.pallas_call(
        kern,
        grid=(nb, nh // heads, nq),
        in_specs=[pl.BlockSpec(memory_space=pltpu.SMEM),
                  pl.BlockSpec((blk, width), qmap),
                  pl.BlockSpec((seq, width), kmap),
                  pl.BlockSpec((seq, width), kmap),
                  pl.BlockSpec((blk, width), qmap)],
        out_specs=pl.BlockSpec((blk, width), qmap),
        out_shape=jax.ShapeDtypeStruct((m, d), BF16),
        compiler_params=_params("arbitrary", "arbitrary", "arbitrary"),
        name="sb_prompt",
    )(bias, q, k, v, g)


def _wide_page(buf, first, page, ntile):
    ht = buf.shape[1]
    flat = buf.reshape(buf.shape[0] * ht, HEAD_B)
    return jnp.concatenate([flat[pl.ds((first + t * page) * ht + j, page, stride=ht), :].astype(BF16)
                            for t in range(ntile) for j in range(ht)], axis=1)


def _sbs_kernel(pt_ref, qbd_ref, bias_ref, kn_ref, vn_ref, g_ref, ck_ref, cv_ref, o_ref,
                acc_ref, carry_ref, kbuf, vbuf, ksem, vsem, *, ts, page, nh, n_pages, group):
    b = pl.program_id(0)
    nb = pl.num_programs(0)
    ht = kbuf.shape[1]
    ntile = nh // ht
    cols = nh * ts
    qbd = qbd_ref[...]
    bias = bias_ref[...]
    ri = lax.broadcasted_iota(jnp.int32, (page, page), 0)
    ci = lax.broadcasted_iota(jnp.int32, (page, page), 1)
    suffix = jnp.where(ci > ri, 1.0, 0.0).astype(BF16)

    def first_row(gp):
        return lax.rem(gp, SBS_SLOTS) * (ntile * page)

    def page_copies(gp):
        seq_i = lax.div(gp, n_pages)
        phys = pt_ref[seq_i, n_pages - 1 - lax.rem(gp, n_pages)]
        slot = lax.rem(gp, SBS_SLOTS)
        out = []
        for t in range(ntile):
            rows = pl.ds(first_row(gp) + t * page, page)
            heads = pl.ds(t * ht, ht)
            out.append(pltpu.make_async_copy(ck_ref.at[phys, :, heads, :], kbuf.at[rows], ksem.at[slot]))
            out.append(pltpu.make_async_copy(cv_ref.at[phys, :, heads, :], vbuf.at[rows], vsem.at[slot]))
        return out

    def start_page(gp):
        @pl.when(gp < nb * n_pages)
        def _():
            for cp in page_copies(gp):
                cp.start()

    @pl.when(b == 0)
    def _():
        for gp in range(SBS_AHEAD):
            start_page(jnp.int32(gp))

    def process(ks, vs, mask):
        zs = [jnp.dot(k, qbd, preferred_element_type=F32) + bias for k in ks]
        sps = [_softplus2(z) for z in zs]
        lks = [-sp if mask is None else jnp.where(mask, -sp, 0.0) for sp in sps]
        rests = [_dot(suffix, lk, pb=SB_CUM_PIECES) for lk in lks]
        carry = carry_ref[...]
        atts = []
        for z, sp, lk, rs in zip(zs, sps, lks, rests):
            att = jnp.exp2(z - sp + rs + carry)
            if mask is not None:
                att = jnp.where(mask, att, 0.0)
            atts.append(att.astype(BF16))
            carry = carry + jnp.sum(lk, axis=0, keepdims=True)
        att = atts[0] if len(atts) == 1 else jnp.concatenate(atts, axis=0)
        val = vs[0] if len(vs) == 1 else jnp.concatenate(vs, axis=0)
        acc_ref[...] += lax.dot_general(att, val, TN, preferred_element_type=F32)
        carry_ref[...] = carry

    acc_ref[...] = jnp.zeros_like(acc_ref)
    carry_ref[...] = jnp.zeros_like(carry_ref)
    pad = jnp.zeros((page - ts, kn_ref.shape[-1]), BF16)
    kn = jnp.concatenate([kn_ref[...].astype(BF16), pad], axis=0)
    vn = jnp.concatenate([vn_ref[...].astype(BF16), pad], axis=0)
    krow = lax.broadcasted_iota(jnp.int32, (page, cols), 0)
    qcol = lax.broadcasted_iota(jnp.int32, (page, cols), 1) & (ts - 1)
    process([kn], [vn], krow < qcol)

    def page_group(it, carry):
        gp0 = b * n_pages + it * group
        for j in range(group):
            for cp in page_copies(gp0 + j):
                cp.wait()
        for j in range(group):
            start_page(gp0 + SBS_AHEAD + j)
        process([_wide_page(kbuf, first_row(gp0 + j), page, ntile) for j in range(group)],
                [_wide_page(vbuf, first_row(gp0 + j), page, ntile) for j in range(group)], None)
        return carry

    lax.fori_loop(0, n_pages // group, page_group, 0)

    g = g_ref[...]
    for h in range(nh):
        sl = slice(h * HEAD_B, (h + 1) * HEAD_B)
        o = acc_ref[h * ts:(h + 1) * ts, sl]
        o_ref[:, sl] = (o * _silu(g[:, sl])).astype(o_ref.dtype)


def _sb_sample(q, k_new, v_new, g, cache_k, cache_v, page_table, bias, ts):
    m, d = q.shape
    nb = m // ts
    nh = d // HEAD_B
    n_pool, page = cache_k.shape[:2]
    n_pages = page_table.shape[1]
    cols = nh * ts
    assert ts % SUBLANES == 0 and ts & (ts - 1) == 0
    scale2 = LOG2E / math.sqrt(HEAD_B)
    q4 = (q * scale2).reshape(nb, ts, nh, HEAD_B)
    eye = jnp.eye(nh, dtype=F32)
    qbd = jnp.einsum("bihc,hg->bhcgi", q4, eye).reshape(nb, d, cols).astype(BF16)
    bias_cols = jnp.repeat(bias * LOG2E, ts).reshape(1, cols)
    ht = math.gcd(nh, SUBLANES)
    ntile = nh // ht
    group = math.gcd(n_pages, SBS_PAGES)
    assert SBS_SLOTS % group == 0 and SBS_AHEAD % group == 0 and SBS_AHEAD + group <= SBS_SLOTS
    tok = lambda b, pt: (b, 0, 0)
    ring = pltpu.VMEM((SBS_SLOTS * ntile * page, ht, HEAD_B), F32)
    grid_spec = pltpu.PrefetchScalarGridSpec(
        num_scalar_prefetch=1,
        grid=(nb,),
        in_specs=[pl.BlockSpec((None, d, cols), tok),
                  pl.BlockSpec((1, cols), lambda b, pt: (0, 0)),
                  pl.BlockSpec((None, ts, d), tok),
                  pl.BlockSpec((None, ts, d), tok),
                  pl.BlockSpec((None, ts, d), tok),
                  pl.BlockSpec(memory_space=pl.ANY),
                  pl.BlockSpec(memory_space=pl.ANY)],
        out_specs=pl.BlockSpec((None, ts, d), tok),
        scratch_shapes=[pltpu.VMEM((cols, d), F32), pltpu.VMEM((1, cols), F32), ring, ring,
                        pltpu.SemaphoreType.DMA((SBS_SLOTS,)), pltpu.SemaphoreType.DMA((SBS_SLOTS,))],
    )
    kern = functools.partial(_sbs_kernel, ts=ts, page=page, nh=nh, n_pages=n_pages, group=group)
    out = pl.pallas_call(
        kern,
        grid_spec=grid_spec,
        out_shape=jax.ShapeDtypeStruct((nb, ts, d), F32),
        compiler_params=_params("arbitrary"),
        name="sb_sample",
    )(page_table, qbd, bias_cols, k_new.reshape(nb, ts, d), v_new.reshape(nb, ts, d), g.reshape(nb, ts, d),
      cache_k, cache_v)
    return out.reshape(m, d)


def _pad_lora(w_in, w_out):
    r = w_in.shape[1]
    w_in = jnp.pad(w_in, ((0, 0), (0, LORA_PAD - r))).astype(BF16)
    w_out = jnp.pad(w_out, ((0, LORA_PAD - r), (0, 0))).astype(BF16)
    return w_in, w_out


def kernel(x_prompt, x_sample, state_shift, state_wkv, cache_k, cache_v, page_table, a_norm, a_mu, a_w_rkvg, a_w0, a_w1, a_w2, a_a0, a_a1, a_a2, a_k_k, a_k_a, a_r_k, a_lnx_g, a_lnx_b, a_w_o, kv_norm, w_kv, b_norm, b_w_qg, b_logit_bias, b_w_o, final_norm):
    nbp, tp, d = x_prompt.shape
    nbs, ts, _ = x_sample.shape
    n_a = a_norm.shape[0]
    n_b = b_norm.shape[0]
    nh_a = d // HEAD_A
    nh_b = d // HEAD_B

    hp = x_prompt.reshape(nbp * tp, d)
    hs = x_sample.reshape(nbs * ts, d)
    shift_p, wkv_p, shift_s, wkv_s = [], [], [], []

    for i in range(n_a):
        w1, w2 = _pad_lora(a_w1[i], a_w2[i])
        a1, a2 = _pad_lora(a_a1[i], a_a2[i])
        w_rkvg = a_w_rkvg[i]
        w_o = a_w_o[i]
        r_k = a_r_k[i].reshape(d)

        def a_layer(h_in, seq, shift0, s0):
            hn, dx, lw, a = _norm_shift_lora(h_in, shift0, seq, a_norm[i], a_mu[i], w1, w2, a_w0[i],
                                              a1, a2, a_a0[i])
            h_last = hn.reshape(-1, seq, d)[:, -1]
            rkvg = _proj(hn, dx, a_mu[i], w_rkvg)
            og, s_out = _wkv(rkvg, lw, a, s0.astype(F32), seq,
                         a_k_k[i], a_k_a[i], r_k, a_lnx_g[i], a_lnx_b[i])
            h_out = _mm_res(og, w_o, h_in)
            return h_out, h_last, s_out.astype(s0.dtype)

        hp, sh, st = a_layer(hp, tp, jnp.zeros((nbp, d), F32),
                             jnp.zeros((nbp,) + state_wkv.shape[2:], state_wkv.dtype))
        shift_p.append(sh); wkv_p.append(st)
        hs, sh, st = a_layer(hs, ts, state_shift[i], state_wkv[i])
        shift_s.append(sh); wkv_s.append(st)

    k_p, k_p16 = _norm_mm(hp, kv_norm, w_kv, 0, d, with_bf16=True)
    v_p, v_p16 = _norm_mm(hp, kv_norm, w_kv, 1, d, with_bf16=True)
    k_s = _norm_mm(hs, kv_norm, w_kv, 0, d)
    v_s = _norm_mm(hs, kv_norm, w_kv, 1, d)

    for j in range(n_b):
        w_qg = b_w_qg[j]
        w_o = b_w_o[j]
        gain = final_norm if j == n_b - 1 else None
        qp = _norm_mm(hp, b_norm[j], w_qg, 0, d)
        gp = _norm_mm(hp, b_norm[j], w_qg, 1, d)
        op = _sb_prompt(qp, k_p16, v_p16, gp, b_logit_bias[j], tp)
        hp = _mm_res(op, w_o, hp, gain)
        qs = _norm_mm(hs, b_norm[j], w_qg, 0, d)
        gs = _norm_mm(hs, b_norm[j], w_qg, 1, d)
        os_ = _sb_sample(qs, k_s, v_s, gs, cache_k, cache_v, page_table, b_logit_bias[j], ts)
        hs = _mm_res(os_, w_o, hs, gain)

    if n_b == 0:
        raise NotImplementedError("trunk without stick-breaking layers")

    return (hp.reshape(nbp, tp, d), hs.reshape(nbs, ts, d),
            jnp.stack(shift_p), jnp.stack(wkv_p),
            k_p.reshape(nbp, tp, nh_b, HEAD_B), v_p.reshape(nbp, tp, nh_b, HEAD_B),
            jnp.stack(shift_s), jnp.stack(wkv_s),
            k_s.reshape(nbs, ts, nh_b, HEAD_B), v_s.reshape(nbs, ts, nh_b, HEAD_B))
```

```python
import functools
import math

import jax
import jax.numpy as jnp
from jax import lax
from jax.experimental import pallas as pl
from jax.experimental.pallas import tpu as pltpu

F32 = jnp.float32
BF16 = jnp.bfloat16

HEAD_A = 64
HEAD_A_SHIFT = 6
HEAD_B = 128
LANES = 128
SUBLANES = 8
LOG2E = 1.4426950408889634
RMS_EPS = 1e-6
LNX_EPS = 64e-5
LORA_PAD = 128
WKV_CHUNK = 64
WKV_SUBCHUNKS = 4
WKV_PAIRS = 16
WKV_P_MX = 1
WKV_P_INV = 1
WKV_P_APPLY = 1
WKV_P_STATE = 1
VMEM_LIMIT = 56 * 1024 * 1024
ROW_TILE = 512
COL_TILE = 512

NN = (((1,), (0,)), ((), ()))
NT = (((1,), (1,)), ((), ()))
TN = (((0,), (0,)), ((), ()))


def _split(x, n):
    parts = []
    for _ in range(n - 1):
        hi = x.astype(BF16)
        parts.append(hi)
        x = x - hi.astype(F32)
    parts.append(x.astype(BF16))
    return parts


def _dot(a, b, dims=NN, pa=1, pb=1):
    ap = _split(a, pa) if pa > 1 or a.dtype != BF16 else [a]
    bp = _split(b, pb) if pb > 1 or b.dtype != BF16 else [b]
    n = max(pa, pb)
    acc = None
    for j in reversed(range(len(bp))):
        lhs = [ap[i] for i in reversed(range(len(ap))) if i + j < n]
        if dims == TN or len(lhs) == 1:
            terms = [lax.dot_general(x, bp[j], dims, preferred_element_type=F32) for x in lhs]
        else:
            m = a.shape[0]
            t = lax.dot_general(jnp.concatenate(lhs, axis=0), bp[j], dims, preferred_element_type=F32)
            terms = [t[k * m:(k + 1) * m] for k in range(len(lhs))]
        for t in terms:
            acc = t if acc is None else acc + t
    return acc


def _dotp(a, b, dims, level):
    return _dot(a, b, dims, pa=level, pb=level)


def _softplus2(z2):
    return jnp.maximum(z2, 0.0) + jnp.log(1.0 + jnp.exp2(-jnp.abs(z2))) * LOG2E


def _silu(g):
    return g / (1.0 + jnp.exp(-g))


def _rms(x, gain):
    return x * lax.rsqrt(jnp.mean(x * x, axis=-1, keepdims=True) + RMS_EPS) * gain


def _params(*sem):
    return pltpu.CompilerParams(dimension_semantics=sem, vmem_limit_bytes=VMEM_LIMIT)


def _token_shift(x_ref, xh_ref, sh_ref, gain, i, *, bm, seq, per_row_start):
    h = _rms(x_ref[...], gain)
    rolled = pltpu.roll(h, 1, 0)
    row = lax.broadcasted_iota(jnp.int32, (bm, 1), 0)
    if per_row_start:
        h_prev = jnp.where((row & (seq - 1)) == 0, sh_ref[...], rolled)
    else:
        halo = _rms(xh_ref[0], gain)
        first = jnp.where((i * bm) % seq == 0, sh_ref[...], halo[SUBLANES - 1:SUBLANES])
        h_prev = jnp.where(row == 0, first, rolled)
    return h, h_prev - h


def _shift_operands(x, shift, seq, bm, per_row_start, tile_index):
    m, d = x.shape
    nb = m // seq
    if per_row_start:
        sh = jnp.repeat(shift, seq, axis=0)
        sh_spec = pl.BlockSpec((bm, d), lambda *g: (tile_index(*g), 0))
    else:
        sh = shift.reshape(nb, 1, d)
        sh_spec = pl.BlockSpec((None, 1, d), lambda *g: ((tile_index(*g) * bm) // seq, 0, 0))
    xh = x.reshape(m // SUBLANES, SUBLANES, d)
    specs = [pl.BlockSpec((bm, d), lambda *g: (tile_index(*g), 0)),
             pl.BlockSpec((1, SUBLANES, d),
                          lambda *g: (jnp.maximum(tile_index(*g) * (bm // SUBLANES) - 1, 0), 0, 0)),
             sh_spec]
    return [x, xh, sh], specs


def _nsl_kernel(x_ref, xh_ref, sh_ref, gain_ref, mu_ref, w1_ref, w2_ref, w0_ref,
                a1_ref, a2_ref, a0_ref, h_ref, dx_ref, lw_ref, a_ref, *, bm, seq, per_row_start):
    h, dx = _token_shift(x_ref, xh_ref, sh_ref, gain_ref[...], pl.program_id(0),
                         bm=bm, seq=seq, per_row_start=per_row_start)
    h_ref[...] = h
    dx_ref[...] = dx
    x4 = (h + dx * mu_ref[4:5, :]).astype(BF16)
    t = jnp.tanh(jnp.dot(x4, w1_ref[...], preferred_element_type=F32)).astype(BF16)
    wl = jnp.dot(t, w2_ref[...], preferred_element_type=F32) + w0_ref[...]
    lw_ref[...] = -math.exp(-0.5) / (1.0 + jnp.exp(-wl))
    x5 = (h + dx * mu_ref[5:6, :]).astype(BF16)
    u = jnp.dot(x5, a1_ref[...], preferred_element_type=F32).astype(BF16)
    al = jnp.dot(u, a2_ref[...], preferred_element_type=F32) + a0_ref[...]
    a_ref[...] = 1.0 / (1.0 + jnp.exp(-al))


def _shift_tiling(m, seq, row_tile):
    per_row_start = seq < SUBLANES * 2
    bm = m if per_row_start else min(row_tile, seq)
    assert m % bm == 0 and (per_row_start or seq % bm == 0)
    return per_row_start, bm


def _norm_shift_lora(x, shift, seq, gain, mu, w1, w2, w0, a1, a2, a0):
    m, d = x.shape
    per_row_start, bm = _shift_tiling(m, seq, ROW_TILE)
    operands, specs = _shift_operands(x, shift, seq, bm, per_row_start, lambda i: i)
    row = lambda i: (i, 0)
    const = lambda i: (0, 0)
    kern = functools.partial(_nsl_kernel, bm=bm, seq=seq, per_row_start=per_row_start)
    return pl.pallas_call(
        kern,
        grid=(m // bm,),
        in_specs=specs + [
            pl.BlockSpec((1, d), const),
            pl.BlockSpec(mu.shape, const),
            pl.BlockSpec(w1.shape, const),
            pl.BlockSpec(w2.shape, const),
            pl.BlockSpec((1, d), const),
            pl.BlockSpec(a1.shape, const),
            pl.BlockSpec(a2.shape, const),
            pl.BlockSpec((1, d), const),
        ],
        out_specs=[pl.BlockSpec((bm, d), row)] * 4,
        out_shape=[jax.ShapeDtypeStruct((m, d), F32)] * 4,
        compiler_params=_params("arbitrary"),
        name="norm_shift_lora",
    )(*operands, gain.reshape(1, d), mu, w1, w2, w0.reshape(1, d), a1, a2, a0.reshape(1, d))


def _proj_kernel(h_ref, dx_ref, mu_ref, w_ref, o_ref, *scratch):
    p = pl.program_id(0)
    if scratch:
        wb_ref, stage_ref, sem = scratch

        def fetch(q):
            return pltpu.make_async_copy(w_ref.at[q], stage_ref, sem.at[0])

        @pl.when(pl.program_id(1) == 0)
        def _():
            @pl.when(p == 0)
            def _():
                fetch(p).start()

            fetch(p).wait()
            wb_ref[...] = stage_ref[...].astype(BF16)

            @pl.when(p + 1 < pl.num_programs(0))
            def _():
                fetch(p + 1).start()

        w = wb_ref[...]
    else:
        w = w_ref[...].astype(BF16)
    xs = (h_ref[...] + dx_ref[...] * mu_ref[pl.ds(p, 1), :]).astype(BF16)
    o_ref[...] = jnp.dot(xs, w, preferred_element_type=F32)


def _proj(h, dx, mu, w):
    m, d = h.shape
    np_, k, n = w.shape
    bm = min(ROW_TILE, m)
    assert m % bm == 0
    if m == bm and n % COL_TILE == 0:
        grid = (np_, n // COL_TILE)
        x_spec = pl.BlockSpec((m, d), lambda p, j: (0, 0))
        w_spec = pl.BlockSpec((None, k, COL_TILE), lambda p, j: (p, 0, j))
        o_spec = pl.BlockSpec((None, m, COL_TILE), lambda p, j: (p, 0, j))
        scratch = []
    else:
        grid = (np_, m // bm)
        x_spec = pl.BlockSpec((bm, d), lambda p, i: (i, 0))
        w_spec = pl.BlockSpec(memory_space=pl.ANY)
        o_spec = pl.BlockSpec((None, bm, n), lambda p, i: (p, i, 0))
        scratch = [pltpu.VMEM((k, n), BF16), pltpu.VMEM((k, n), F32), pltpu.SemaphoreType.DMA((1,))]
    return pl.pallas_call(
        _proj_kernel,
        grid=grid,
        in_specs=[x_spec, x_spec, pl.BlockSpec(mu.shape, lambda p, t: (0, 0)), w_spec],
        out_specs=o_spec,
        out_shape=jax.ShapeDtypeStruct((np_, m, n), F32),
        scratch_shapes=scratch,
        compiler_params=_params("arbitrary", "arbitrary"),
        name="proj",
    )(h, dx, mu, w)


def _wkv_kernel(r_ref, k_ref, v_ref, g_ref, w_ref, a_ref, kk_ref, ka_ref, rk_ref, lg_ref, lb_ref,
                s0_ref, o_ref, so_ref, y_ref, *, chunk, rows, pairs, sub):
    c = pl.program_id(2)
    nc = pl.num_programs(2)
    C = chunk
    C2 = 2 * C

    @pl.when(c == 0)
    def _():
        zero = jnp.zeros((HEAD_A, HEAD_A), F32)
        for p in range(pairs):
            top = jnp.concatenate([s0_ref[2 * p], zero], axis=1)
            bot = jnp.concatenate([zero, s0_ref[2 * p + 1]], axis=1)
            y_ref[p] = jnp.concatenate([top, bot], axis=0)

    lane = lax.broadcasted_iota(jnp.int32, (1, LANES), 1)
    head0 = lane < HEAD_A
    ri = lax.broadcasted_iota(jnp.int32, (C2, C2), 0)
    ci = lax.broadcasted_iota(jnp.int32, (C2, C2), 1)
    same_head = (lax.broadcasted_iota(jnp.int32, (LANES, LANES), 0) >> HEAD_A_SHIFT) == (
        lax.broadcasted_iota(jnp.int32, (LANES, LANES), 1) >> HEAD_A_SHIFT)
    si = ri & (C - 1)
    sj = ci & (C - 1)
    strict = sj < si
    incl = sj <= si
    eye = jnp.where(ri == ci, 1.0, 0.0)
    trow = lax.broadcasted_iota(jnp.int32, (C, 1), 0)

    def stack(x):
        return jnp.concatenate([jnp.where(head0, x, 0.0), jnp.where(head0, 0.0, x)], axis=0)

    def pack(x):
        return x[:C] + x[C:]

    def head_sum(x):
        s0 = jnp.sum(jnp.where(head0, x, 0.0), axis=1, keepdims=True)
        s1 = jnp.sum(jnp.where(head0, 0.0, x), axis=1, keepdims=True)
        return jnp.where(head0, s0, s1)

    def pair_chain(p, tok):
        ln = slice(p * LANES, (p + 1) * LANES)

        def load(ref):
            x = ref[tok, ln]
            if rows < C:
                x = jnp.concatenate([x, jnp.zeros((C - rows, LANES), F32)], axis=0)
            return x

        R, K, V, Wl, A = (load(ref) for ref in (r_ref, k_ref, v_ref, w_ref, a_ref))

        kkp = K * kk_ref[:, ln]
        kk = kkp / jnp.maximum(jnp.sqrt(head_sum(kkp * kkp)), 1e-12)
        Km = K * (1.0 + (A - 1.0) * ka_ref[:, ln])

        Lc = Wl
        shift = 1
        while shift < C:
            Lc = Lc + jnp.where(trow >= shift, pltpu.roll(Lc, shift, 0), 0.0)
            shift *= 2
        yield
        Lend = Lc[C - 1:C, :]
        g_inv = jnp.exp(-Lc)
        g_rem = jnp.exp(Lend - Lc)
        kka = kk * A
        Ah = -kk * jnp.exp(Lc - Wl)
        Rh = R * jnp.exp(Lc)
        Bc = kka * g_inv
        Kc = Km * g_inv
        Bt = kka * g_rem
        Kt = Km * g_rem

        sAh = stack(Ah)
        lhs = jnp.concatenate([sAh, stack(Rh)], axis=0)
        rhs = jnp.concatenate([stack(Bc), stack(Kc)], axis=0)
        mx = _dotp(lhs, rhs, NT, WKV_P_MX)
        yield
        Nab =jnp.where(strict, mx[:C2, :C2], 0.0)
        nrest = jnp.concatenate([jnp.where(strict, mx[:C2, C2:], 0.0),
                                 jnp.where(incl, mx[C2:, C2:], 0.0)], axis=0)
        Nrb = jnp.where(incl, mx[C2:, :C2], 0.0)

        Tm = eye + Nab
        Pw = Nab
        nv = _dotp(nrest, stack(V), NN, WKV_P_APPLY)
        yield
        nsq = int(math.log2(C)) - 1
        Pw = _dotp(Pw, Pw, NN, WKV_P_INV)
        yield
        for it in range(nsq):
            if it < nsq - 1:
                both = _dotp(jnp.concatenate([Pw, Tm], axis=0), Pw, NN, WKV_P_INV)
                Pw, Tm = both[:C2], Tm + both[C2:]
            else:
                Tm = Tm + _dotp(Tm, Pw, NN, WKV_P_INV)
            yield

        tx = _dotp(Tm, jnp.concatenate([sAh, nv[:C2]], axis=1), NN, WKV_P_APPLY)
        yield

        Y = y_ref[p]
        sy = _dotp(jnp.concatenate([tx[:, :LANES], Rh], axis=0), Y, NT, WKV_P_STATE)
        yield
        Ub = sy[:C2] + tx[:, LANES:]
        U = pack(Ub)
        O = sy[C2:] + pack(_dotp(Nrb, Ub, NN, WKV_P_STATE) + nv[C2:])
        upd = _dotp(jnp.concatenate([U, V], axis=0), jnp.concatenate([Bt, Kt], axis=0), TN, WKV_P_STATE)
        yield
        y_new = Y * jnp.exp(Lend) + jnp.where(same_head, upd, 0.0)

        mean = head_sum(O) * (1.0 / HEAD_A)
        dlt = O - mean
        var = head_sum(dlt * dlt) * (1.0 / HEAD_A)
        on = dlt * lax.rsqrt(var + LNX_EPS) * lg_ref[:, ln] + lb_ref[:, ln]
        bonus = head_sum(R * Km * rk_ref[:, ln]) * V
        res = (on + bonus) * _silu(load(g_ref))
        yield
        y_ref[p] = y_new
        o_ref[tok, ln] = res[:rows].astype(o_ref.dtype)

    def chunk_step(s, carry):
        tok = pl.ds(pl.multiple_of(s * rows, rows), rows)
        chains = [pair_chain(p, tok) for p in range(pairs)]
        while chains:
            chains = [ch for ch in chains if next(ch, True) is None]
        return carry

    lax.fori_loop(0, sub, chunk_step, 0)

    @pl.when(c == nc - 1)
    def _():
        for p in range(pairs):
            y = y_ref[p]
            so_ref[2 * p] = y[:HEAD_A, :HEAD_A]
            so_ref[2 * p + 1] = y[HEAD_A:, HEAD_A:]


def _wkv(rkvg, lw, a, s0, seq, k_k, k_a, r_k, lnx_g, lnx_b):
    _, m, d = rkvg.shape
    nb = m // seq
    npair = d // LANES
    pairs = math.gcd(npair, WKV_PAIRS)
    width = pairs * LANES
    rows = min(WKV_CHUNK, seq)
    assert seq % rows == 0 and rows % SUBLANES == 0
    sub = math.gcd(seq // rows, WKV_SUBCHUNKS)
    blk = sub * rows
    nc = seq // blk
    tok = lambda b, h, c: (b * nc + c, h)
    par = lambda b, h, c: (0, h)
    st = lambda b, h, c: (b, h, 0, 0)

    def proj(p):
        return pl.BlockSpec((None, blk, width), lambda b, h, c: (p, b * nc + c, h))

    vec = lambda x: x.reshape(1, d)
    chunk = min(WKV_CHUNK, max(SUBLANES, rows))
    kern = functools.partial(_wkv_kernel, chunk=chunk, rows=rows, pairs=pairs, sub=sub)
    return pl.pallas_call(
        kern,
        grid=(nb, npair // pairs, nc),
        in_specs=[proj(0), proj(1), proj(2), proj(3),
                  pl.BlockSpec((blk, width), tok), pl.BlockSpec((blk, width), tok),
                  pl.BlockSpec((1, width), par), pl.BlockSpec((1, width), par),
                  pl.BlockSpec((1, width), par), pl.BlockSpec((1, width), par),
                  pl.BlockSpec((1, width), par),
                  pl.BlockSpec((None, 2 * pairs, HEAD_A, HEAD_A), st)],
        out_specs=[pl.BlockSpec((blk, width), tok),
                   pl.BlockSpec((None, 2 * pairs, HEAD_A, HEAD_A), st)],
        out_shape=[jax.ShapeDtypeStruct((m, d), BF16 if rows % (2 * SUBLANES) == 0 else F32),
                   jax.ShapeDtypeStruct((nb, 2 * npair, HEAD_A, HEAD_A), F32)],
        scratch_shapes=[pltpu.VMEM((pairs, LANES, LANES), F32)],
        compiler_params=_params("arbitrary", "arbitrary", "arbitrary"),
        name="wkv",
    )(rkvg, rkvg, rkvg, rkvg, lw, a, vec(k_k), vec(k_a), vec(r_k), vec(lnx_g), vec(lnx_b), s0)


def _cast_weight_once(w_ref, wb_ref):
    @pl.when(pl.program_id(0) == 0)
    def _():
        wb_ref[...] = w_ref[...].astype(BF16)


def _weight_spec(k, n, col):
    return pl.BlockSpec((k, n), lambda i: (0, col), pipeline_mode=pl.Buffered(1))


def _mm_res_kernel(x_ref, w_ref, r_ref, g_ref, o_ref, wb_ref, *, final_norm):
    _cast_weight_once(w_ref, wb_ref)
    y = r_ref[...] + jnp.dot(x_ref[...].astype(BF16), wb_ref[...], preferred_element_type=F32)
    if final_norm:
        y = _rms(y, g_ref[...])
    o_ref[...] = y


def _mm_res_cols_kernel(x_ref, w_ref, r_ref, g_ref, o_ref, *, final_norm):
    j = pl.program_id(0)
    cols = pl.ds(pl.multiple_of(j * COL_TILE, COL_TILE), COL_TILE)
    o_ref[:, cols] = r_ref[:, cols] + jnp.dot(x_ref[...].astype(BF16), w_ref[...].astype(BF16),
                                              preferred_element_type=F32)
    if final_norm:
        @pl.when(j == pl.num_programs(0) - 1)
        def _():
            o_ref[...] = _rms(o_ref[...], g_ref[...])


def _mm_res(x, w, res, gain=None):
    m, k = x.shape
    n = w.shape[1]
    bm = min(ROW_TILE, m)
    assert m % bm == 0
    final_norm = gain is not None
    g = (gain if final_norm else jnp.ones((n,), F32)).reshape(1, n)
    if m == bm and n % COL_TILE == 0:
        whole = lambda j: (0, 0)
        return pl.pallas_call(
            functools.partial(_mm_res_cols_kernel, final_norm=final_norm),
            grid=(n // COL_TILE,),
            in_specs=[pl.BlockSpec((m, k), whole),
                      pl.BlockSpec((k, COL_TILE), lambda j: (0, j)),
                      pl.BlockSpec((m, n), whole),
                      pl.BlockSpec((1, n), whole)],
            out_specs=pl.BlockSpec((m, n), whole),
            out_shape=jax.ShapeDtypeStruct((m, n), F32),
            compiler_params=_params("arbitrary"),
            name="mm_res_cols",
        )(x, w, res, g)
    return pl.pallas_call(
        functools.partial(_mm_res_kernel, final_norm=final_norm),
        grid=(m // bm,),
        in_specs=[pl.BlockSpec((bm, k), lambda i: (i, 0)),
                  _weight_spec(k, n, 0),
                  pl.BlockSpec((bm, n), lambda i: (i, 0)),
                  pl.BlockSpec((1, n), lambda i: (0, 0))],
        out_specs=pl.BlockSpec((bm, n), lambda i: (i, 0)),
        out_shape=jax.ShapeDtypeStruct((m, n), F32),
        scratch_shapes=[pltpu.VMEM((k, n), BF16)],
        compiler_params=_params("arbitrary"),
        name="mm_res",
    )(x, w, res, g)


def _norm_mm_kernel(x_ref, g_ref, w_ref, *refs, resident_weight, n_out):
    outs = refs[:n_out]
    if resident_weight:
        wb_ref = refs[n_out]
        _cast_weight_once(w_ref, wb_ref)
        w = wb_ref[...]
    else:
        w = w_ref[...].astype(BF16)
    xn = _rms(x_ref[...], g_ref[...]).astype(BF16)
    y = jnp.dot(xn, w, preferred_element_type=F32)
    for o_ref in outs:
        o_ref[...] = y.astype(o_ref.dtype)


def _norm_mm(x, gain, w, col, n, with_bf16=False):
    m, k = x.shape
    bm = min(ROW_TILE, m)
    assert m % bm == 0 and w.shape[1] % n == 0
    dtypes = [F32, BF16] if with_bf16 else [F32]
    out_shape = [jax.ShapeDtypeStruct((m, n), dt) for dt in dtypes]
    if m == bm and n % COL_TILE == 0:
        nj = n // COL_TILE
        outs = pl.pallas_call(
            functools.partial(_norm_mm_kernel, resident_weight=False, n_out=len(dtypes)),
            grid=(nj,),
            in_specs=[pl.BlockSpec((m, k), lambda j: (0, 0)),
                      pl.BlockSpec((1, k), lambda j: (0, 0)),
                      pl.BlockSpec((k, COL_TILE), lambda j: (0, col * nj + j))],
            out_specs=[pl.BlockSpec((m, COL_TILE), lambda j: (0, j)) for _ in dtypes],
            out_shape=out_shape,
            compiler_params=_params("arbitrary"),
            name="norm_mm_cols",
        )(x, gain.reshape(1, k), w)
    else:
        outs = pl.pallas_call(
            functools.partial(_norm_mm_kernel, resident_weight=True, n_out=len(dtypes)),
            grid=(m // bm,),
            in_specs=[pl.BlockSpec((bm, k), lambda i: (i, 0)),
                      pl.BlockSpec((1, k), lambda i: (0, 0)),
                      _weight_spec(k, n, col)],
            out_specs=[pl.BlockSpec((bm, n), lambda i: (i, 0)) for _ in dtypes],
            out_shape=out_shape,
            scratch_shapes=[pltpu.VMEM((k, n), BF16)],
            compiler_params=_params("arbitrary"),
            name="norm_mm",
        )(x, gain.reshape(1, k), w)
    return tuple(outs) if with_bf16 else outs[0]


SB_BLOCK = 256
SB_HEADS = 8
SB_CUM_PIECES = 1
SBS_PAGES = 4
SBS_SLOTS = 16
SBS_AHEAD = 12


def _sbp_kernel(bias_ref, q_ref, kb_ref, vb_ref, g_ref, o_ref, *, blk, scale, heads):
    hg = pl.program_id(1)
    qi = pl.program_id(2)
    lanes = [slice(h * HEAD_B, (h + 1) * HEAD_B) for h in range(heads)]
    biases = [bias_ref[hg * heads + h] * LOG2E for h in range(heads)]
    qs = [(q_ref[:, ln] * (scale * LOG2E)).astype(BF16) for ln in lanes]
    def suffix_ones(n):
        return jnp.where(lax.broadcasted_iota(jnp.int32, (n, n), 0) > lax.broadcasted_iota(jnp.int32, (n, n), 1),
                         1.0, 0.0).astype(BF16)

    def tile_chain(h, q, keys, carry, mask, suffix, tot, val):
        ln = lanes[h]
        z = lax.dot_general(q, kb_ref[keys, ln], NT, preferred_element_type=F32) + biases[h]
        yield
        sp = _softplus2(z)
        lk = -sp if mask is None else jnp.where(mask, -sp, 0.0)
        log_beta = z - sp
        tot[h] = jnp.sum(lk, axis=1, keepdims=True)
        yield
        rs = _dot(lk, suffix, pa=SB_CUM_PIECES)
        yield
        e = log_beta + rs
        att = jnp.exp2(e if carry is None else e + carry())
        if mask is not None:
            att = jnp.where(mask, att, 0.0)
        yield
        val[h] = jnp.dot(att.astype(BF16), vb_ref[keys, ln], preferred_element_type=F32)

    def run(chains):
        while chains:
            chains = [ch for ch in chains if next(ch, True) is None]

    suffix = suffix_ones(blk)

    causal = lax.broadcasted_iota(jnp.int32, (blk, blk), 1) < lax.broadcasted_iota(jnp.int32, (blk, blk), 0)
    carries, accs = [None] * heads, [None] * heads
    diag_keys = pl.ds(pl.multiple_of(qi * blk, blk), blk)
    run([tile_chain(h, qs[h], diag_keys, None, causal, suffix, carries, accs) for h in range(heads)])

    def block(j, carries, accs):
        keys = pl.ds(pl.multiple_of(j * blk, blk), blk)
        tot, val = [None] * heads, [None] * heads
        run([tile_chain(h, qs[h], keys, functools.partial(carries.__getitem__, h), None, suffix, tot, val)
             for h in range(heads)])
        return [c + t for c, t in zip(carries, tot)], [a + v for a, v in zip(accs, val)]

    def body(it, ca):
        cs, as_ = block(qi - 1 - it, list(ca[:heads]), list(ca[heads:]))
        return tuple(cs) + tuple(as_)

    ca = lax.fori_loop(0, qi, body, tuple(carries) + tuple(accs))
    for h, ln in enumerate(lanes):
        o_ref[:, ln] = (ca[heads + h] * _silu(g_ref[:, ln])).astype(o_ref.dtype)


def _sb_prompt(q, k, v, g, bias, seq):
    m, d = q.shape
    nb = m // seq
    nh = d // HEAD_B
    heads = math.gcd(nh, SB_HEADS)
    width = heads * HEAD_B
    blk = min(SB_BLOCK, seq)
    assert seq % blk == 0
    nq = seq // blk
    qmap = lambda b, h, i: (b * nq + i, h)
    kmap = lambda b, h, i: (b, h)
    kern = functools.partial(_sbp_kernel, blk=blk, scale=1.0 / math.sqrt(HEAD_B), heads=heads)
    return pl.pallas_call(
        kern,
        grid=(nb, nh // heads, nq),
        in_specs=[pl.BlockSpec(memory_space=pltpu.SMEM),
                  pl.BlockSpec((blk, width), qmap),
                  pl.BlockSpec((seq, width), kmap),
                  pl.BlockSpec((seq, width), kmap),
                  pl.BlockSpec((blk, width), qmap)],
        out_specs=pl.BlockSpec((blk, width), qmap),
        out_shape=jax.ShapeDtypeStruct((m, d), BF16),
        compiler_params=_params("arbitrary", "arbitrary", "arbitrary"),
        name="sb_prompt",
    )(bias, q, k, v, g)


def _wide_page(buf, first, page, ntile):
    ht = buf.shape[1]
    flat = buf.reshape(buf.shape[0] * ht, HEAD_B)
    return jnp.concatenate([flat[pl.ds((first + t * page) * ht + j, page, stride=ht), :].astype(BF16)
                            for t in range(ntile) for j in range(ht)], axis=1)


def _sbs_kernel(pt_ref, qbd_ref, bias_ref, kn_ref, vn_ref, g_ref, ck_ref, cv_ref, o_ref,
                acc_ref, carry_ref, kbuf, vbuf, ksem, vsem, *, ts, page, nh, n_pages, group):
    b = pl.program_id(0)
    nb = pl.num_programs(0)
    ht = kbuf.shape[1]
    ntile = nh // ht
    cols = nh * ts
    qbd = qbd_ref[...]
    bias = bias_ref[...]
    ri = lax.broadcasted_iota(jnp.int32, (page, page), 0)
    ci = lax.broadcasted_iota(jnp.int32, (page, page), 1)
    suffix = jnp.where(ci > ri, 1.0, 0.0).astype(BF16)

    def first_row(gp):
        return lax.rem(gp, SBS_SLOTS) * (ntile * page)

    def page_copies(gp):
        seq_i = lax.div(gp, n_pages)
        phys = pt_ref[seq_i, n_pages - 1 - lax.rem(gp, n_pages)]
        slot = lax.rem(gp, SBS_SLOTS)
        out = []
        for t in range(ntile):
            rows = pl.ds(first_row(gp) + t * page, page)
            heads = pl.ds(t * ht, ht)
            out.append(pltpu.make_async_copy(ck_ref.at[phys, :, heads, :], kbuf.at[rows], ksem.at[slot]))
            out.append(pltpu.make_async_copy(cv_ref.at[phys, :, heads, :], vbuf.at[rows], vsem.at[slot]))
        return out

    def start_page(gp):
        @pl.when(gp < nb * n_pages)
        def _():
            for cp in page_copies(gp):
                cp.start()

    @pl.when(b == 0)
    def _():
        for gp in range(SBS_AHEAD):
            start_page(jnp.int32(gp))

    def process(ks, vs, mask):
        zs = [jnp.dot(k, qbd, preferred_element_type=F32) + bias for k in ks]
        sps = [_softplus2(z) for z in zs]
        lks = [-sp if mask is None else jnp.where(mask, -sp, 0.0) for sp in sps]
        rests = [_dot(suffix, lk, pb=SB_CUM_PIECES) for lk in lks]
        carry = carry_ref[...]
        atts = []
        for z, sp, lk, rs in zip(zs, sps, lks, rests):
            att = jnp.exp2(z - sp + rs + carry)
            if mask is not None:
                att = jnp.where(mask, att, 0.0)
            atts.append(att.astype(BF16))
            carry = carry + jnp.sum(lk, axis=0, keepdims=True)
        att = atts[0] if len(atts) == 1 else jnp.concatenate(atts, axis=0)
        val = vs[0] if len(vs) == 1 else jnp.concatenate(vs, axis=0)
        acc_ref[...] += lax.dot_general(att, val, TN, preferred_element_type=F32)
        carry_ref[...] = carry

    acc_ref[...] = jnp.zeros_like(acc_ref)
    carry_ref[...] = jnp.zeros_like(carry_ref)
    pad = jnp.zeros((page - ts, kn_ref.shape[-1]), BF16)
    kn = jnp.concatenate([kn_ref[...].astype(BF16), pad], axis=0)
    vn = jnp.concatenate([vn_ref[...].astype(BF16), pad], axis=0)
    krow = lax.broadcasted_iota(jnp.int32, (page, cols), 0)
    qcol = lax.broadcasted_iota(jnp.int32, (page, cols), 1) & (ts - 1)
    process([kn], [vn], krow < qcol)

    def page_group(it, carry):
        gp0 = b * n_pages + it * group
        for j in range(group):
            for cp in page_copies(gp0 + j):
                cp.wait()
        for j in range(group):
            start_page(gp0 + SBS_AHEAD + j)
        process([_wide_page(kbuf, first_row(gp0 + j), page, ntile) for j in range(group)],
                [_wide_page(vbuf, first_row(gp0 + j), page, ntile) for j in range(group)], None)
        return carry

    lax.fori_loop(0, n_pages // group, page_group, 0)

    g = g_ref[...]
    for h in range(nh):
        sl = slice(h * HEAD_B, (h + 1) * HEAD_B)
        o = acc_ref[h * ts:(h + 1) * ts, sl]
        o_ref[:, sl] = (o * _silu(g[:, sl])).astype(o_ref.dtype)


def _sb_sample(q, k_new, v_new, g, cache_k, cache_v, page_table, bias, ts):
    m, d = q.shape
    nb = m // ts
    nh = d // HEAD_B
    n_pool, page = cache_k.shape[:2]
    n_pages = page_table.shape[1]
    cols = nh * ts
    assert ts % SUBLANES == 0 and ts & (ts - 1) == 0
    scale2 = LOG2E / math.sqrt(HEAD_B)
    q4 = (q * scale2).reshape(nb, ts, nh, HEAD_B)
    eye = jnp.eye(nh, dtype=F32)
    qbd = jnp.einsum("bihc,hg->bhcgi", q4, eye).reshape(nb, d, cols).astype(BF16)
    bias_cols = jnp.repeat(bias * LOG2E, ts).reshape(1, cols)
    ht = math.gcd(nh, SUBLANES)
    ntile = nh // ht
    group = math.gcd(n_pages, SBS_PAGES)
    assert SBS_SLOTS % group == 0 and SBS_AHEAD % group == 0 and SBS_AHEAD + group <= SBS_SLOTS
    tok = lambda b, pt: (b, 0, 0)
    ring = pltpu.VMEM((SBS_SLOTS * ntile * page, ht, HEAD_B), F32)
    grid_spec = pltpu.PrefetchScalarGridSpec(
        num_scalar_prefetch=1,
        grid=(nb,),
        in_specs=[pl.BlockSpec((None, d, cols), tok),
                  pl.BlockSpec((1, cols), lambda b, pt: (0, 0)),
                  pl.BlockSpec((None, ts, d), tok),
                  pl.BlockSpec((None, ts, d), tok),
                  pl.BlockSpec((None, ts, d), tok),
                  pl.BlockSpec(memory_space=pl.ANY),
                  pl.BlockSpec(memory_space=pl.ANY)],
        out_specs=pl.BlockSpec((None, ts, d), tok),
        scratch_shapes=[pltpu.VMEM((cols, d), F32), pltpu.VMEM((1, cols), F32), ring, ring,
                        pltpu.SemaphoreType.DMA((SBS_SLOTS,)), pltpu.SemaphoreType.DMA((SBS_SLOTS,))],
    )
    kern = functools.partial(_sbs_kernel, ts=ts, page=page, nh=nh, n_pages=n_pages, group=group)
    out = pl.pallas_call(
        kern,
        grid_spec=grid_spec,
        out_shape=jax.ShapeDtypeStruct((nb, ts, d), F32),
        compiler_params=_params("arbitrary"),
        name="sb_sample",
    )(page_table, qbd, bias_cols, k_new.reshape(nb, ts, d), v_new.reshape(nb, ts, d), g.reshape(nb, ts, d),
      cache_k, cache_v)
    return out.reshape(m, d)


def _pad_lora(w_in, w_out):
    r = w_in.shape[1]
    w_in = jnp.pad(w_in, ((0, 0), (0, LORA_PAD - r))).astype(BF16)
    w_out = jnp.pad(w_out, ((0, LORA_PAD - r), (0, 0))).astype(BF16)
    return w_in, w_out


def kernel(x_prompt, x_sample, state_shift, state_wkv, cache_k, cache_v, page_table, a_norm, a_mu, a_w_rkvg, a_w0, a_w1, a_w2, a_a0, a_a1, a_a2, a_k_k, a_k_a, a_r_k, a_lnx_g, a_lnx_b, a_w_o, kv_norm, w_kv, b_norm, b_w_qg, b_logit_bias, b_w_o, final_norm):
    nbp, tp, d = x_prompt.shape
    nbs, ts, _ = x_sample.shape
    n_a = a_norm.shape[0]
    n_b = b_norm.shape[0]
    nh_a = d // HEAD_A
    nh_b = d // HEAD_B

    hp = x_prompt.reshape(nbp * tp, d)
    hs = x_sample.reshape(nbs * ts, d)
    shift_p, wkv_p, shift_s, wkv_s = [], [], [], []

    for i in range(n_a):
        w1, w2 = _pad_lora(a_w1[i], a_w2[i])
        a1, a2 = _pad_lora(a_a1[i], a_a2[i])
        w_rkvg = a_w_rkvg[i]
        w_o = a_w_o[i]
        r_k = a_r_k[i].reshape(d)

        def a_layer(h_in, seq, shift0, s0):
            hn, dx, lw, a = _norm_shift_lora(h_in, shift0, seq, a_norm[i], a_mu[i], w1, w2, a_w0[i],
                                              a1, a2, a_a0[i])
            h_last = hn.reshape(-1, seq, d)[:, -1]
            rkvg = _proj(hn, dx, a_mu[i], w_rkvg)
            og, s_out = _wkv(rkvg, lw, a, s0.astype(F32), seq,
                         a_k_k[i], a_k_a[i], r_k, a_lnx_g[i], a_lnx_b[i])
            h_out = _mm_res(og, w_o, h_in)
            return h_out, h_last, s_out.astype(s0.dtype)

        hp, sh, st = a_layer(hp, tp, jnp.zeros((nbp, d), F32),
                             jnp.zeros((nbp,) + state_wkv.shape[2:], state_wkv.dtype))
        shift_p.append(sh); wkv_p.append(st)
        hs, sh, st = a_layer(hs, ts, state_shift[i], state_wkv[i])
        shift_s.append(sh); wkv_s.append(st)

    k_p, k_p16 = _norm_mm(hp, kv_norm, w_kv, 0, d, with_bf16=True)
    v_p, v_p16 = _norm_mm(hp, kv_norm, w_kv, 1, d, with_bf16=True)
    k_s = _norm_mm(hs, kv_norm, w_kv, 0, d)
    v_s = _norm_mm(hs, kv_norm, w_kv, 1, d)

    for j in range(n_b):
        w_qg = b_w_qg[j]
        w_o = b_w_o[j]
        gain = final_norm if j == n_b - 1 else None
        qp = _norm_mm(hp, b_norm[j], w_qg, 0, d)
        gp = _norm_mm(hp, b_norm[j], w_qg, 1, d)
        op = _sb_prompt(qp, k_p16, v_p16, gp, b_logit_bias[j], tp)
        hp = _mm_res(op, w_o, hp, gain)
        qs = _norm_mm(hs, b_norm[j], w_qg, 0, d)
        gs = _norm_mm(hs, b_norm[j], w_qg, 1, d)
        os_ = _sb_sample(qs, k_s, v_s, gs, cache_k, cache_v, page_table, b_logit_bias[j], ts)
        hs = _mm_res(os_, w_o, hs, gain)

    if n_b == 0:
        raise NotImplementedError("trunk without stick-breaking layers")

    return (hp.reshape(nbp, tp, d), hs.reshape(nbs, ts, d),
            jnp.stack(shift_p), jnp.stack(wkv_p),
            k_p.reshape(nbp, tp, nh_b, HEAD_B), v_p.reshape(nbp, tp, nh_b, HEAD_B),
            jnp.stack(shift_s), jnp.stack(wkv_s),
            k_s.reshape(nbs, ts, nh_b, HEAD_B), v_s.reshape(nbs, ts, nh_b, HEAD_B))
```

```python
import functools
import math

import jax
import jax.numpy as jnp
from jax import lax
from jax.experimental import pallas as pl
from jax.experimental.pallas import tpu as pltpu

F32 = jnp.float32
BF16 = jnp.bfloat16

HEAD_A = 64
HEAD_A_SHIFT = 6
HEAD_B = 128
LANES = 128
SUBLANES = 8
LOG2E = 1.4426950408889634
RMS_EPS = 1e-6
LNX_EPS = 64e-5
LORA_PAD = 128
WKV_CHUNK = 64
WKV_SUBCHUNKS = 4
WKV_PAIRS = 16
WKV_P_MX = 1
WKV_P_INV = 1
WKV_P_APPLY = 1
WKV_P_STATE = 1
VMEM_LIMIT = 56 * 1024 * 1024
ROW_TILE = 512
COL_TILE = 512

NN = (((1,), (0,)), ((), ()))
NT = (((1,), (1,)), ((), ()))
TN = (((0,), (0,)), ((), ()))


def _split(x, n):
    parts = []
    for _ in range(n - 1):
        hi = x.astype(BF16)
        parts.append(hi)
        x = x - hi.astype(F32)
    parts.append(x.astype(BF16))
    return parts


def _dot(a, b, dims=NN, pa=1, pb=1):
    ap = _split(a, pa) if pa > 1 or a.dtype != BF16 else [a]
    bp = _split(b, pb) if pb > 1 or b.dtype != BF16 else [b]
    n = max(pa, pb)
    acc = None
    for j in reversed(range(len(bp))):
        lhs = [ap[i] for i in reversed(range(len(ap))) if i + j < n]
        if dims == TN or len(lhs) == 1:
            terms = [lax.dot_general(x, bp[j], dims, preferred_element_type=F32) for x in lhs]
        else:
            m = a.shape[0]
            t = lax.dot_general(jnp.concatenate(lhs, axis=0), bp[j], dims, preferred_element_type=F32)
            terms = [t[k * m:(k + 1) * m] for k in range(len(lhs))]
        for t in terms:
            acc = t if acc is None else acc + t
    return acc


def _dotp(a, b, dims, level):
    return _dot(a, b, dims, pa=level, pb=level)


def _softplus2(z2):
    return jnp.maximum(z2, 0.0) + jnp.log(1.0 + jnp.exp2(-jnp.abs(z2))) * LOG2E


def _silu(g):
    return g / (1.0 + jnp.exp(-g))


def _rms(x, gain):
    return x * lax.rsqrt(jnp.mean(x * x, axis=-1, keepdims=True) + RMS_EPS) * gain


def _params(*sem):
    return pltpu.CompilerParams(dimension_semantics=sem, vmem_limit_bytes=VMEM_LIMIT)


def _token_shift(x_ref, xh_ref, sh_ref, gain, i, *, bm, seq, per_row_start):
    h = _rms(x_ref[...], gain)
    rolled = pltpu.roll(h, 1, 0)
    row = lax.broadcasted_iota(jnp.int32, (bm, 1), 0)
    if per_row_start:
        h_prev = jnp.where((row & (seq - 1)) == 0, sh_ref[...], rolled)
    else:
        halo = _rms(xh_ref[0], gain)
        first = jnp.where((i * bm) % seq == 0, sh_ref[...], halo[SUBLANES - 1:SUBLANES])
        h_prev = jnp.where(row == 0, first, rolled)
    return h, h_prev - h


def _shift_operands(x, shift, seq, bm, per_row_start, tile_index):
    m, d = x.shape
    nb = m // seq
    if per_row_start:
        sh = jnp.repeat(shift, seq, axis=0)
        sh_spec = pl.BlockSpec((bm, d), lambda *g: (tile_index(*g), 0))
    else:
        sh = shift.reshape(nb, 1, d)
        sh_spec = pl.BlockSpec((None, 1, d), lambda *g: ((tile_index(*g) * bm) // seq, 0, 0))
    xh = x.reshape(m // SUBLANES, SUBLANES, d)
    specs = [pl.BlockSpec((bm, d), lambda *g: (tile_index(*g), 0)),
             pl.BlockSpec((1, SUBLANES, d),
                          lambda *g: (jnp.maximum(tile_index(*g) * (bm // SUBLANES) - 1, 0), 0, 0)),
             sh_spec]
    return [x, xh, sh], specs


def _nsl_kernel(x_ref, xh_ref, sh_ref, gain_ref, mu_ref, w1_ref, w2_ref, w0_ref,
                a1_ref, a2_ref, a0_ref, h_ref, dx_ref, lw_ref, a_ref, *, bm, seq, per_row_start):
    h, dx = _token_shift(x_ref, xh_ref, sh_ref, gain_ref[...], pl.program_id(0),
                         bm=bm, seq=seq, per_row_start=per_row_start)
    h_ref[...] = h
    dx_ref[...] = dx
    x4 = (h + dx * mu_ref[4:5, :]).astype(BF16)
    t = jnp.tanh(jnp.dot(x4, w1_ref[...], preferred_element_type=F32)).astype(BF16)
    wl = jnp.dot(t, w2_ref[...], preferred_element_type=F32) + w0_ref[...]
    lw_ref[...] = -math.exp(-0.5) / (1.0 + jnp.exp(-wl))
    x5 = (h + dx * mu_ref[5:6, :]).astype(BF16)
    u = jnp.dot(x5, a1_ref[...], preferred_element_type=F32).astype(BF16)
    al = jnp.dot(u, a2_ref[...], preferred_element_type=F32) + a0_ref[...]
    a_ref[...] = 1.0 / (1.0 + jnp.exp(-al))


def _shift_tiling(m, seq, row_tile):
    per_row_start = seq < SUBLANES * 2
    bm = m if per_row_start else min(row_tile, seq)
    assert m % bm == 0 and (per_row_start or seq % bm == 0)
    return per_row_start, bm


def _norm_shift_lora(x, shift, seq, gain, mu, w1, w2, w0, a1, a2, a0):
    m, d = x.shape
    per_row_start, bm = _shift_tiling(m, seq, ROW_TILE)
    operands, specs = _shift_operands(x, shift, seq, bm, per_row_start, lambda i: i)
    row = lambda i: (i, 0)
    const = lambda i: (0, 0)
    kern = functools.partial(_nsl_kernel, bm=bm, seq=seq, per_row_start=per_row_start)
    return pl.pallas_call(
        kern,
        grid=(m // bm,),
        in_specs=specs + [
            pl.BlockSpec((1, d), const),
            pl.BlockSpec(mu.shape, const),
            pl.BlockSpec(w1.shape, const),
            pl.BlockSpec(w2.shape, const),
            pl.BlockSpec((1, d), const),
            pl.BlockSpec(a1.shape, const),
            pl.BlockSpec(a2.shape, const),
            pl.BlockSpec((1, d), const),
        ],
        out_specs=[pl.BlockSpec((bm, d), row)] * 4,
        out_shape=[jax.ShapeDtypeStruct((m, d), F32)] * 4,
        compiler_params=_params("arbitrary"),
        name="norm_shift_lora",
    )(*operands, gain.reshape(1, d), mu, w1, w2, w0.reshape(1, d), a1, a2, a0.reshape(1, d))


def _proj_kernel(h_ref, dx_ref, mu_ref, w_ref, o_ref, *scratch):
    p = pl.program_id(0)
    if scratch:
        wb_ref, stage_ref, sem = scratch

        def fetch(q):
            return pltpu.make_async_copy(w_ref.at[q], stage_ref, sem.at[0])

        @pl.when(pl.program_id(1) == 0)
        def _():
            @pl.when(p == 0)
            def _():
                fetch(p).start()

            fetch(p).wait()
            wb_ref[...] = stage_ref[...].astype(BF16)

            @pl.when(p + 1 < pl.num_programs(0))
            def _():
                fetch(p + 1).start()

        w = wb_ref[...]
    else:
        w = w_ref[...].astype(BF16)
    xs = (h_ref[...] + dx_ref[...] * mu_ref[pl.ds(p, 1), :]).astype(BF16)
    o_ref[...] = jnp.dot(xs, w, preferred_element_type=F32)


def _proj(h, dx, mu, w):
    m, d = h.shape
    np_, k, n = w.shape
    bm = min(ROW_TILE, m)
    assert m % bm == 0
    if m == bm and n % COL_TILE == 0:
        grid = (np_, n // COL_TILE)
        x_spec = pl.BlockSpec((m, d), lambda p, j: (0, 0))
        w_spec = pl.BlockSpec((None, k, COL_TILE), lambda p, j: (p, 0, j))
        o_spec = pl.BlockSpec((None, m, COL_TILE), lambda p, j: (p, 0, j))
        scratch = []
    else:
        grid = (np_, m // bm)
        x_spec = pl.BlockSpec((bm, d), lambda p, i: (i, 0))
        w_spec = pl.BlockSpec(memory_space=pl.ANY)
        o_spec = pl.BlockSpec((None, bm, n), lambda p, i: (p, i, 0))
        scratch = [pltpu.VMEM((k, n), BF16), pltpu.VMEM((k, n), F32), pltpu.SemaphoreType.DMA((1,))]
    return pl.pallas_call(
        _proj_kernel,
        grid=grid,
        in_specs=[x_spec, x_spec, pl.BlockSpec(mu.shape, lambda p, t: (0, 0)), w_spec],
        out_specs=o_spec,
        out_shape=jax.ShapeDtypeStruct((np_, m, n), F32),
        scratch_shapes=scratch,
        compiler_params=_params("arbitrary", "arbitrary"),
        name="proj",
    )(h, dx, mu, w)


def _wkv_kernel(r_ref, k_ref, v_ref, g_ref, w_ref, a_ref, kk_ref, ka_ref, rk_ref, lg_ref, lb_ref,
                s0_ref, o_ref, so_ref, y_ref, *, chunk, rows, pairs, sub):
    c = pl.program_id(2)
    nc = pl.num_programs(2)
    C = chunk
    C2 = 2 * C

    @pl.when(c == 0)
    def _():
        zero = jnp.zeros((HEAD_A, HEAD_A), F32)
        for p in range(pairs):
            top = jnp.concatenate([s0_ref[2 * p], zero], axis=1)
            bot = jnp.concatenate([zero, s0_ref[2 * p + 1]], axis=1)
            y_ref[p] = jnp.concatenate([top, bot], axis=0)

    lane = lax.broadcasted_iota(jnp.int32, (1, LANES), 1)
    head0 = lane < HEAD_A
    ri = lax.broadcasted_iota(jnp.int32, (C2, C2), 0)
    ci = lax.broadcasted_iota(jnp.int32, (C2, C2), 1)
    same_head = (lax.broadcasted_iota(jnp.int32, (LANES, LANES), 0) >> HEAD_A_SHIFT) == (
        lax.broadcasted_iota(jnp.int32, (LANES, LANES), 1) >> HEAD_A_SHIFT)
    si = ri & (C - 1)
    sj = ci & (C - 1)
    strict = sj < si
    incl = sj <= si
    eye = jnp.where(ri == ci, 1.0, 0.0)
    trow = lax.broadcasted_iota(jnp.int32, (C, 1), 0)

    def stack(x):
        return jnp.concatenate([jnp.where(head0, x, 0.0), jnp.where(head0, 0.0, x)], axis=0)

    def pack(x):
        return x[:C] + x[C:]

    def head_sum(x):
        s0 = jnp.sum(jnp.where(head0, x, 0.0), axis=1, keepdims=True)
        s1 = jnp.sum(jnp.where(head0, 0.0, x), axis=1, keepdims=True)
        return jnp.where(head0, s0, s1)

    def pair_chain(p, tok):
        ln = slice(p * LANES, (p + 1) * LANES)

        def load(ref):
            x = ref[tok, ln]
            if rows < C:
                x = jnp.concatenate([x, jnp.zeros((C - rows, LANES), F32)], axis=0)
            return x

        R, K, V, Wl, A = (load(ref) for ref in (r_ref, k_ref, v_ref, w_ref, a_ref))

        kkp = K * kk_ref[:, ln]
        kk = kkp / jnp.maximum(jnp.sqrt(head_sum(kkp * kkp)), 1e-12)
        Km = K * (1.0 + (A - 1.0) * ka_ref[:, ln])

        Lc = Wl
        shift = 1
        while shift < C:
            Lc = Lc + jnp.where(trow >= shift, pltpu.roll(Lc, shift, 0), 0.0)
            shift *= 2
        yield
        Lend = Lc[C - 1:C, :]
        g_inv = jnp.exp(-Lc)
        g_rem = jnp.exp(Lend - Lc)
        kka = kk * A
        Ah = -kk * jnp.exp(Lc - Wl)
        Rh = R * jnp.exp(Lc)
        Bc = kka * g_inv
        Kc = Km * g_inv
        Bt = kka * g_rem
        Kt = Km * g_rem

        sAh = stack(Ah)
        lhs = jnp.concatenate([sAh, stack(Rh)], axis=0)
        rhs = jnp.concatenate([stack(Bc), stack(Kc)], axis=0)
        mx = _dotp(lhs, rhs, NT, WKV_P_MX)
        yield
        Nab =jnp.where(strict, mx[:C2, :C2], 0.0)
        nrest = jnp.concatenate([jnp.where(strict, mx[:C2, C2:], 0.0),
                                 jnp.where(incl, mx[C2:, C2:], 0.0)], axis=0)
        Nrb = jnp.where(incl, mx[C2:, :C2], 0.0)

        Tm = eye + Nab
        Pw = Nab
        nv = _dotp(nrest, stack(V), NN, WKV_P_APPLY)
        yield
        nsq = int(math.log2(C)) - 1
        Pw = _dotp(Pw, Pw, NN, WKV_P_INV)
        yield
        for it in range(nsq):
            if it < nsq - 1:
                both = _dotp(jnp.concatenate([Pw, Tm], axis=0), Pw, NN, WKV_P_INV)
                Pw, Tm = both[:C2], Tm + both[C2:]
            else:
                Tm = Tm + _dotp(Tm, Pw, NN, WKV_P_INV)
            yield

        tx = _dotp(Tm, jnp.concatenate([sAh, nv[:C2]], axis=1), NN, WKV_P_APPLY)
        yield

        Y = y_ref[p]
        sy = _dotp(jnp.concatenate([tx[:, :LANES], Rh], axis=0), Y, NT, WKV_P_STATE)
        yield
        Ub = sy[:C2] + tx[:, LANES:]
        U = pack(Ub)
        O = sy[C2:] + pack(_dotp(Nrb, Ub, NN, WKV_P_STATE) + nv[C2:])
        upd = _dotp(jnp.concatenate([U, V], axis=0), jnp.concatenate([Bt, Kt], axis=0), TN, WKV_P_STATE)
        yield
        y_new = Y * jnp.exp(Lend) + jnp.where(same_head, upd, 0.0)

        mean = head_sum(O) * (1.0 / HEAD_A)
        dlt = O - mean
        var = head_sum(dlt * dlt) * (1.0 / HEAD_A)
        on = dlt * lax.rsqrt(var + LNX_EPS) * lg_ref[:, ln] + lb_ref[:, ln]
        bonus = head_sum(R * Km * rk_ref[:, ln]) * V
        res = (on + bonus) * _silu(load(g_ref))
        yield
        y_ref[p] = y_new
        o_ref[tok, ln] = res[:rows].astype(o_ref.dtype)

    def chunk_step(s, carry):
        tok = pl.ds(pl.multiple_of(s * rows, rows), rows)
        chains = [pair_chain(p, tok) for p in range(pairs)]
        while chains:
            chains = [ch for ch in chains if next(ch, True) is None]
        return carry

    lax.fori_loop(0, sub, chunk_step, 0)

    @pl.when(c == nc - 1)
    def _():
        for p in range(pairs):
            y = y_ref[p]
            so_ref[2 * p] = y[:HEAD_A, :HEAD_A]
            so_ref[2 * p + 1] = y[HEAD_A:, HEAD_A:]


def _wkv(rkvg, lw, a, s0, seq, k_k, k_a, r_k, lnx_g, lnx_b):
    _, m, d = rkvg.shape
    nb = m // seq
    npair = d // LANES
    pairs = math.gcd(npair, WKV_PAIRS)
    width = pairs * LANES
    rows = min(WKV_CHUNK, seq)
    assert seq % rows == 0 and rows % SUBLANES == 0
    sub = math.gcd(seq // rows, WKV_SUBCHUNKS)
    blk = sub * rows
    nc = seq // blk
    tok = lambda b, h, c: (b * nc + c, h)
    par = lambda b, h, c: (0, h)
    st = lambda b, h, c: (b, h, 0, 0)

    def proj(p):
        return pl.BlockSpec((None, blk, width), lambda b, h, c: (p, b * nc + c, h))

    vec = lambda x: x.reshape(1, d)
    chunk = min(WKV_CHUNK, max(SUBLANES, rows))
    kern = functools.partial(_wkv_kernel, chunk=chunk, rows=rows, pairs=pairs, sub=sub)
    return pl.pallas_call(
        kern,
        grid=(nb, npair // pairs, nc),
        in_specs=[proj(0), proj(1), proj(2), proj(3),
                  pl.BlockSpec((blk, width), tok), pl.BlockSpec((blk, width), tok),
                  pl.BlockSpec((1, width), par), pl.BlockSpec((1, width), par),
                  pl.BlockSpec((1, width), par), pl.BlockSpec((1, width), par),
                  pl.BlockSpec((1, width), par),
                  pl.BlockSpec((None, 2 * pairs, HEAD_A, HEAD_A), st)],
        out_specs=[pl.BlockSpec((blk, width), tok),
                   pl.BlockSpec((None, 2 * pairs, HEAD_A, HEAD_A), st)],
        out_shape=[jax.ShapeDtypeStruct((m, d), BF16 if rows % (2 * SUBLANES) == 0 else F32),
                   jax.ShapeDtypeStruct((nb, 2 * npair, HEAD_A, HEAD_A), F32)],
        scratch_shapes=[pltpu.VMEM((pairs, LANES, LANES), F32)],
        compiler_params=_params("arbitrary", "arbitrary", "arbitrary"),
        name="wkv",
    )(rkvg, rkvg, rkvg, rkvg, lw, a, vec(k_k), vec(k_a), vec(r_k), vec(lnx_g), vec(lnx_b), s0)


def _cast_weight_once(w_ref, wb_ref):
    @pl.when(pl.program_id(0) == 0)
    def _():
        wb_ref[...] = w_ref[...].astype(BF16)


def _weight_spec(k, n, col):
    return pl.BlockSpec((k, n), lambda i: (0, col), pipeline_mode=pl.Buffered(1))


def _mm_res_kernel(x_ref, w_ref, r_ref, g_ref, o_ref, wb_ref, *, final_norm):
    _cast_weight_once(w_ref, wb_ref)
    y = r_ref[...] + jnp.dot(x_ref[...].astype(BF16), wb_ref[...], preferred_element_type=F32)
    if final_norm:
        y = _rms(y, g_ref[...])
    o_ref[...] = y


def _mm_res_cols_kernel(x_ref, w_ref, r_ref, g_ref, o_ref, *, final_norm):
    j = pl.program_id(0)
    cols = pl.ds(pl.multiple_of(j * COL_TILE, COL_TILE), COL_TILE)
    o_ref[:, cols] = r_ref[:, cols] + jnp.dot(x_ref[...].astype(BF16), w_ref[...].astype(BF16),
                                              preferred_element_type=F32)
    if final_norm:
        @pl.when(j == pl.num_programs(0) - 1)
        def _():
            o_ref[...] = _rms(o_ref[...], g_ref[...])


def _mm_res(x, w, res, gain=None):
    m, k = x.shape
    n = w.shape[1]
    bm = min(ROW_TILE, m)
    assert m % bm == 0
    final_norm = gain is not None
    g = (gain if final_norm else jnp.ones((n,), F32)).reshape(1, n)
    if m == bm and n % COL_TILE == 0:
        whole = lambda j: (0, 0)
        return pl.pallas_call(
            functools.partial(_mm_res_cols_kernel, final_norm=final_norm),
            grid=(n // COL_TILE,),
            in_specs=[pl.BlockSpec((m, k), whole),
                      pl.BlockSpec((k, COL_TILE), lambda j: (0, j)),
                      pl.BlockSpec((m, n), whole),
                      pl.BlockSpec((1, n), whole)],
            out_specs=pl.BlockSpec((m, n), whole),
            out_shape=jax.ShapeDtypeStruct((m, n), F32),
            compiler_params=_params("arbitrary"),
            name="mm_res_cols",
        )(x, w, res, g)
    return pl.pallas_call(
        functools.partial(_mm_res_kernel, final_norm=final_norm),
        grid=(m // bm,),
        in_specs=[pl.BlockSpec((bm, k), lambda i: (i, 0)),
                  _weight_spec(k, n, 0),
                  pl.BlockSpec((bm, n), lambda i: (i, 0)),
                  pl.BlockSpec((1, n), lambda i: (0, 0))],
        out_specs=pl.BlockSpec((bm, n), lambda i: (i, 0)),
        out_shape=jax.ShapeDtypeStruct((m, n), F32),
        scratch_shapes=[pltpu.VMEM((k, n), BF16)],
        compiler_params=_params("arbitrary"),
        name="mm_res",
    )(x, w, res, g)


def _norm_mm_kernel(x_ref, g_ref, w_ref, *refs, resident_weight, n_out):
    outs = refs[:n_out]
    if resident_weight:
        wb_ref = refs[n_out]
        _cast_weight_once(w_ref, wb_ref)
        w = wb_ref[...]
    else:
        w = w_ref[...].astype(BF16)
    xn = _rms(x_ref[...], g_ref[...]).astype(BF16)
    y = jnp.dot(xn, w, preferred_element_type=F32)
    for o_ref in outs:
        o_ref[...] = y.astype(o_ref.dtype)


def _norm_mm(x, gain, w, col, n, with_bf16=False):
    m, k = x.shape
    bm = min(ROW_TILE, m)
    assert m % bm == 0 and w.shape[1] % n == 0
    dtypes = [F32, BF16] if with_bf16 else [F32]
    out_shape = [jax.ShapeDtypeStruct((m, n), dt) for dt in dtypes]
    if m == bm and n % COL_TILE == 0:
        nj = n // COL_TILE
        outs = pl.pallas_call(
            functools.partial(_norm_mm_kernel, resident_weight=False, n_out=len(dtypes)),
            grid=(nj,),
            in_specs=[pl.BlockSpec((m, k), lambda j: (0, 0)),
                      pl.BlockSpec((1, k), lambda j: (0, 0)),
                      pl.BlockSpec((k, COL_TILE), lambda j: (0, col * nj + j))],
            out_specs=[pl.BlockSpec((m, COL_TILE), lambda j: (0, j)) for _ in dtypes],
            out_shape=out_shape,
            compiler_params=_params("arbitrary"),
            name="norm_mm_cols",
        )(x, gain.reshape(1, k), w)
    else:
        outs = pl.pallas_call(
            functools.partial(_norm_mm_kernel, resident_weight=True, n_out=len(dtypes)),
            grid=(m // bm,),
            in_specs=[pl.BlockSpec((bm, k), lambda i: (i, 0)),
                      pl.BlockSpec((1, k), lambda i: (0, 0)),
                      _weight_spec(k, n, col)],
            out_specs=[pl.BlockSpec((bm, n), lambda i: (i, 0)) for _ in dtypes],
            out_shape=out_shape,
            scratch_shapes=[pltpu.VMEM((k, n), BF16)],
            compiler_params=_params("arbitrary"),
            name="norm_mm",
        )(x, gain.reshape(1, k), w)
    return tuple(outs) if with_bf16 else outs[0]


SB_BLOCK = 256
SB_HEADS = 8
SB_CUM_PIECES = 1
SBS_PAGES = 4
SBS_SLOTS = 16
SBS_AHEAD = 12


def _sbp_kernel(bias_ref, q_ref, kb_ref, vb_ref, g_ref, o_ref, *, blk, scale, heads):
    hg = pl.program_id(1)
    qi = pl.program_id(2)
    lanes = [slice(h * HEAD_B, (h + 1) * HEAD_B) for h in range(heads)]
    biases = [bias_ref[hg * heads + h] * LOG2E for h in range(heads)]
    qs = [(q_ref[:, ln] * (scale * LOG2E)).astype(BF16) for ln in lanes]
    def suffix_ones(n):
        return jnp.where(lax.broadcasted_iota(jnp.int32, (n, n), 0) > lax.broadcasted_iota(jnp.int32, (n, n), 1),
                         1.0, 0.0).astype(BF16)

    def tile_chain(h, q, keys, carry, mask, suffix, tot, val):
        ln = lanes[h]
        z = lax.dot_general(q, kb_ref[keys, ln], NT, preferred_element_type=F32) + biases[h]
        yield
        sp = _softplus2(z)
        lk = -sp if mask is None else jnp.where(mask, -sp, 0.0)
        log_beta = z - sp
        tot[h] = jnp.sum(lk, axis=1, keepdims=True)
        yield
        rs = _dot(lk, suffix, pa=SB_CUM_PIECES)
        yield
        e = log_beta + rs
        att = jnp.exp2(e if carry is None else e + carry())
        if mask is not None:
            att = jnp.where(mask, att, 0.0)
        yield
        val[h] = jnp.dot(att.astype(BF16), vb_ref[keys, ln], preferred_element_type=F32)

    def run(chains):
        while chains:
            chains = [ch for ch in chains if next(ch, True) is None]

    suffix = suffix_ones(blk)

    causal = lax.broadcasted_iota(jnp.int32, (blk, blk), 1) < lax.broadcasted_iota(jnp.int32, (blk, blk), 0)
    carries, accs = [None] * heads, [None] * heads
    diag_keys = pl.ds(pl.multiple_of(qi * blk, blk), blk)
    run([tile_chain(h, qs[h], diag_keys, None, causal, suffix, carries, accs) for h in range(heads)])

    def block(j, carries, accs):
        keys = pl.ds(pl.multiple_of(j * blk, blk), blk)
        tot, val = [None] * heads, [None] * heads
        run([tile_chain(h, qs[h], keys, functools.partial(carries.__getitem__, h), None, suffix, tot, val)
             for h in range(heads)])
        return [c + t for c, t in zip(carries, tot)], [a + v for a, v in zip(accs, val)]

    def body(it, ca):
        cs, as_ = block(qi - 1 - it, list(ca[:heads]), list(ca[heads:]))
        return tuple(cs) + tuple(as_)

    ca = lax.fori_loop(0, qi, body, tuple(carries) + tuple(accs))
    for h, ln in enumerate(lanes):
        o_ref[:, ln] = (ca[heads + h] * _silu(g_ref[:, ln])).astype(o_ref.dtype)


def _sb_prompt(q, k, v, g, bias, seq):
    m, d = q.shape
    nb = m // seq
    nh = d // HEAD_B
    heads = math.gcd(nh, SB_HEADS)
    width = heads * HEAD_B
    blk = min(SB_BLOCK, seq)
    assert seq % blk == 0
    nq = seq // blk
    qmap = lambda b, h, i: (b * nq + i, h)
    kmap = lambda b, h, i: (b, h)
    kern = functools.partial(_sbp_kernel, blk=blk, scale=1.0 / math.sqrt(HEAD_B), heads=heads)
    return pl.pallas_call(
        kern,
        grid=(nb, nh // heads, nq),
        in_specs=[pl.BlockSpec(memory_space=pltpu.SMEM),
                  pl.BlockSpec((blk, width), qmap),
                  pl.BlockSpec((seq, width), kmap),
                  pl.BlockSpec((seq, width), kmap),
                  pl.BlockSpec((blk, width), qmap)],
        out_specs=pl.BlockSpec((blk, width), qmap),
        out_shape=jax.ShapeDtypeStruct((m, d), BF16),
        compiler_params=_params("arbitrary", "arbitrary", "arbitrary"),
        name="sb_prompt",
    )(bias, q, k, v, g)


def _wide_page(buf, first, page, ntile):
    ht = buf.shape[1]
    flat = buf.reshape(buf.shape[0] * ht, HEAD_B)
    return jnp.concatenate([flat[pl.ds((first + t * page) * ht + j, page, stride=ht), :].astype(BF16)
                            for t in range(ntile) for j in range(ht)], axis=1)


def _sbs_kernel(pt_ref, qbd_ref, bias_ref, kn_ref, vn_ref, g_ref, ck_ref, cv_ref, o_ref,
                acc_ref, carry_ref, kbuf, vbuf, ksem, vsem, *, ts, page, nh, n_pages, group):
    b = pl.program_id(0)
    nb = pl.num_programs(0)
    ht = kbuf.shape[1]
    ntile = nh // ht
    cols = nh * ts
    qbd = qbd_ref[...]
    bias = bias_ref[...]
    ri = lax.broadcasted_iota(jnp.int32, (page, page), 0)
    ci = lax.broadcasted_iota(jnp.int32, (page, page), 1)
    suffix = jnp.where(ci > ri, 1.0, 0.0).astype(BF16)

    def first_row(gp):
        return lax.rem(gp, SBS_SLOTS) * (ntile * page)

    def page_copies(gp):
        seq_i = lax.div(gp, n_pages)
        phys = pt_ref[seq_i, n_pages - 1 - lax.rem(gp, n_pages)]
        slot = lax.rem(gp, SBS_SLOTS)
        out = []
        for t in range(ntile):
            rows = pl.ds(first_row(gp) + t * page, page)
            heads = pl.ds(t * ht, ht)
            out.append(pltpu.make_async_copy(ck_ref.at[phys, :, heads, :], kbuf.at[rows], ksem.at[slot]))
            out.append(pltpu.make_async_copy(cv_ref.at[phys, :, heads, :], vbuf.at[rows], vsem.at[slot]))
        return out

    def start_page(gp):
        @pl.when(gp < nb * n_pages)
        def _():
            for n, cp in enumerate(page_copies(gp)):
                cp.start(priority=n % 2)

    @pl.when(b == 0)
    def _():
        for gp in range(SBS_AHEAD):
            start_page(jnp.int32(gp))

    def process(ks, vs, mask):
        zs = [jnp.dot(k, qbd, preferred_element_type=F32) + bias for k in ks]
        sps = [_softplus2(z) for z in zs]
        lks = [-sp if mask is None else jnp.where(mask, -sp, 0.0) for sp in sps]
        rests = [_dot(suffix, lk, pb=SB_CUM_PIECES) for lk in lks]
        carry = carry_ref[...]
        atts = []
        for z, sp, lk, rs in zip(zs, sps, lks, rests):
            att = jnp.exp2(z - sp + rs + carry)
            if mask is not None:
                att = jnp.where(mask, att, 0.0)
            atts.append(att.astype(BF16))
            carry = carry + jnp.sum(lk, axis=0, keepdims=True)
        att = atts[0] if len(atts) == 1 else jnp.concatenate(atts, axis=0)
        val = vs[0] if len(vs) == 1 else jnp.concatenate(vs, axis=0)
        acc_ref[...] += lax.dot_general(att, val, TN, preferred_element_type=F32)
        carry_ref[...] = carry

    acc_ref[...] = jnp.zeros_like(acc_ref)
    carry_ref[...] = jnp.zeros_like(carry_ref)
    pad = jnp.zeros((page - ts, kn_ref.shape[-1]), BF16)
    kn = jnp.concatenate([kn_ref[...].astype(BF16), pad], axis=0)
    vn = jnp.concatenate([vn_ref[...].astype(BF16), pad], axis=0)
    krow = lax.broadcasted_iota(jnp.int32, (page, cols), 0)
    qcol = lax.broadcasted_iota(jnp.int32, (page, cols), 1) & (ts - 1)
    process([kn], [vn], krow < qcol)

    def page_group(it, carry):
        gp0 = b * n_pages + it * group
        for j in range(group):
            for cp in page_copies(gp0 + j):
                cp.wait()
        for j in range(group):
            start_page(gp0 + SBS_AHEAD + j)
        process([_wide_page(kbuf, first_row(gp0 + j), page, ntile) for j in range(group)],
                [_wide_page(vbuf, first_row(gp0 + j), page, ntile) for j in range(group)], None)
        return carry

    lax.fori_loop(0, n_pages // group, page_group, 0)

    g = g_ref[...]
    for h in range(nh):
        sl = slice(h * HEAD_B, (h + 1) * HEAD_B)
        o = acc_ref[h * ts:(h + 1) * ts, sl]
        o_ref[:, sl] = (o * _silu(g[:, sl])).astype(o_ref.dtype)


def _sb_sample(q, k_new, v_new, g, cache_k, cache_v, page_table, bias, ts):
    m, d = q.shape
    nb = m // ts
    nh = d // HEAD_B
    n_pool, page = cache_k.shape[:2]
    n_pages = page_table.shape[1]
    cols = nh * ts
    assert ts % SUBLANES == 0 and ts & (ts - 1) == 0
    scale2 = LOG2E / math.sqrt(HEAD_B)
    q4 = (q * scale2).reshape(nb, ts, nh, HEAD_B)
    eye = jnp.eye(nh, dtype=F32)
    qbd = jnp.einsum("bihc,hg->bhcgi", q4, eye).reshape(nb, d, cols).astype(BF16)
    bias_cols = jnp.repeat(bias * LOG2E, ts).reshape(1, cols)
    ht = math.gcd(nh, SUBLANES)
    ntile = nh // ht
    group = math.gcd(n_pages, SBS_PAGES)
    assert SBS_SLOTS % group == 0 and SBS_AHEAD % group == 0 and SBS_AHEAD + group <= SBS_SLOTS
    tok = lambda b, pt: (b, 0, 0)
    ring = pltpu.VMEM((SBS_SLOTS * ntile * page, ht, HEAD_B), F32)
    grid_spec = pltpu.PrefetchScalarGridSpec(
        num_scalar_prefetch=1,
        grid=(nb,),
        in_specs=[pl.BlockSpec((None, d, cols), tok),
                  pl.BlockSpec((1, cols), lambda b, pt: (0, 0)),
                  pl.BlockSpec((None, ts, d), tok),
                  pl.BlockSpec((None, ts, d), tok),
                  pl.BlockSpec((None, ts, d), tok),
                  pl.BlockSpec(memory_space=pl.ANY),
                  pl.BlockSpec(memory_space=pl.ANY)],
        out_specs=pl.BlockSpec((None, ts, d), tok),
        scratch_shapes=[pltpu.VMEM((cols, d), F32), pltpu.VMEM((1, cols), F32), ring, ring,
                        pltpu.SemaphoreType.DMA((SBS_SLOTS,)), pltpu.SemaphoreType.DMA((SBS_SLOTS,))],
    )
    kern = functools.partial(_sbs_kernel, ts=ts, page=page, nh=nh, n_pages=n_pages, group=group)
    out = pl.pallas_call(
        kern,
        grid_spec=grid_spec,
        out_shape=jax.ShapeDtypeStruct((nb, ts, d), F32),
        compiler_params=_params("arbitrary"),
        name="sb_sample",
    )(page_table, qbd, bias_cols, k_new.reshape(nb, ts, d), v_new.reshape(nb, ts, d), g.reshape(nb, ts, d),
      cache_k, cache_v)
    return out.reshape(m, d)


def _pad_lora(w_in, w_out):
    r = w_in.shape[1]
    w_in = jnp.pad(w_in, ((0, 0), (0, LORA_PAD - r))).astype(BF16)
    w_out = jnp.pad(w_out, ((0, LORA_PAD - r), (0, 0))).astype(BF16)
    return w_in, w_out


def kernel(x_prompt, x_sample, state_shift, state_wkv, cache_k, cache_v, page_table, a_norm, a_mu, a_w_rkvg, a_w0, a_w1, a_w2, a_a0, a_a1, a_a2, a_k_k, a_k_a, a_r_k, a_lnx_g, a_lnx_b, a_w_o, kv_norm, w_kv, b_norm, b_w_qg, b_logit_bias, b_w_o, final_norm):
    nbp, tp, d = x_prompt.shape
    nbs, ts, _ = x_sample.shape
    n_a = a_norm.shape[0]
    n_b = b_norm.shape[0]
    nh_a = d // HEAD_A
    nh_b = d // HEAD_B

    hp = x_prompt.reshape(nbp * tp, d)
    hs = x_sample.reshape(nbs * ts, d)
    shift_p, wkv_p, shift_s, wkv_s = [], [], [], []

    for i in range(n_a):
        w1, w2 = _pad_lora(a_w1[i], a_w2[i])
        a1, a2 = _pad_lora(a_a1[i], a_a2[i])
        w_rkvg = a_w_rkvg[i]
        w_o = a_w_o[i]
        r_k = a_r_k[i].reshape(d)

        def a_layer(h_in, seq, shift0, s0):
            hn, dx, lw, a = _norm_shift_lora(h_in, shift0, seq, a_norm[i], a_mu[i], w1, w2, a_w0[i],
                                              a1, a2, a_a0[i])
            h_last = hn.reshape(-1, seq, d)[:, -1]
            rkvg = _proj(hn, dx, a_mu[i], w_rkvg)
            og, s_out = _wkv(rkvg, lw, a, s0.astype(F32), seq,
                         a_k_k[i], a_k_a[i], r_k, a_lnx_g[i], a_lnx_b[i])
            h_out = _mm_res(og, w_o, h_in)
            return h_out, h_last, s_out.astype(s0.dtype)

        hp, sh, st = a_layer(hp, tp, jnp.zeros((nbp, d), F32),
                             jnp.zeros((nbp,) + state_wkv.shape[2:], state_wkv.dtype))
        shift_p.append(sh); wkv_p.append(st)
        hs, sh, st = a_layer(hs, ts, state_shift[i], state_wkv[i])
        shift_s.append(sh); wkv_s.append(st)

    k_p, k_p16 = _norm_mm(hp, kv_norm, w_kv, 0, d, with_bf16=True)
    v_p, v_p16 = _norm_mm(hp, kv_norm, w_kv, 1, d, with_bf16=True)
    k_s = _norm_mm(hs, kv_norm, w_kv, 0, d)
    v_s = _norm_mm(hs, kv_norm, w_kv, 1, d)

    for j in range(n_b):
        w_qg = b_w_qg[j]
        w_o = b_w_o[j]
        gain = final_norm if j == n_b - 1 else None
        qp = _norm_mm(hp, b_norm[j], w_qg, 0, d)
        gp = _norm_mm(hp, b_norm[j], w_qg, 1, d)
        op = _sb_prompt(qp, k_p16, v_p16, gp, b_logit_bias[j], tp)
        hp = _mm_res(op, w_o, hp, gain)
        qs = _norm_mm(hs, b_norm[j], w_qg, 0, d)
        gs = _norm_mm(hs, b_norm[j], w_qg, 1, d)
        os_ = _sb_sample(qs, k_s, v_s, gs, cache_k, cache_v, page_table, b_logit_bias[j], ts)
        hs = _mm_res(os_, w_o, hs, gain)

    if n_b == 0:
        raise NotImplementedError("trunk without stick-breaking layers")

    return (hp.reshape(nbp, tp, d), hs.reshape(nbs, ts, d),
            jnp.stack(shift_p), jnp.stack(wkv_p),
            k_p.reshape(nbp, tp, nh_b, HEAD_B), v_p.reshape(nbp, tp, nh_b, HEAD_B),
            jnp.stack(shift_s), jnp.stack(wkv_s),
            k_s.reshape(nbs, ts, nh_b, HEAD_B), v_s.reshape(nbs, ts, nh_b, HEAD_B))
```
